```python
import numpy as np
import jax, jax.numpy as jnp
from jax import lax

D_MODEL = 1024
BATCH = 2
SEQ = 8192
DEPTH = 4

N_HEADS = 4
HEAD_DIM = D_MODEL // 8
MIX_W = N_HEADS * HEAD_DIM
N_BRANCH = 3
CHUNK = 64
GLA_RANK = 16
GLA_TAU = 16
CONV_K = 4
D_FF = 4 * D_MODEL
EPS = 1e-6

kernel_name = 'hybrid_mlstm_gla_gdn_block'


def _in_sizes():
    H, W = N_HEADS, MIX_W
    return [W, W, W, W, H, H,
            W, W, W, W, GLA_RANK,
            W, W, W, W, H, H,
            N_BRANCH * D_MODEL]


def rmsnorm(x, g):
    x32 = x.astype(jnp.float32)
    y = x32 * lax.rsqrt(jnp.mean(x32 * x32, axis=-1, keepdims=True) + EPS)
    return (y * g.astype(jnp.float32)).astype(x.dtype)


def head_rmsnorm(h, g):
    B_, S_, _ = h.shape
    hh = h.reshape(B_, S_, N_HEADS, HEAD_DIM)
    hh = hh * lax.rsqrt(jnp.mean(hh * hh, axis=-1, keepdims=True) + EPS)
    return hh.reshape(B_, S_, MIX_W) * g.astype(jnp.float32)


def l2norm_heads(h):
    B_, S_, _ = h.shape
    hh = h.reshape(B_, S_, N_HEADS, HEAD_DIM)
    hh = hh * lax.rsqrt(jnp.sum(hh * hh, axis=-1, keepdims=True) + EPS)
    return hh.reshape(B_, S_, MIX_W)


def to_chunks(t):
    B_, S_, HD = t.shape
    t = t.reshape(B_, S_ // CHUNK, CHUNK, N_HEADS, HD // N_HEADS)
    return t.transpose(1, 0, 3, 2, 4)


def scalar_chunks(t):
    B_, S_, H_ = t.shape
    return t.reshape(B_, S_ // CHUNK, CHUNK, H_).transpose(1, 0, 3, 2)


def from_chunks(t):
    NC, B_, H_, L_, d = t.shape
    return t.transpose(1, 0, 3, 2, 4).reshape(B_, NC * L_, H_ * d)


def causal_conv(x, w):
    K = w.shape[0]
    xp = jnp.pad(x, ((0, 0), (K - 1, 0), (0, 0)))
    return lax.conv_general_dilated(xp, w[:, None, :], window_strides=(1,), padding='VALID',
                                    dimension_numbers=('NWC', 'WIO', 'NWC'),
                                    feature_group_count=x.shape[-1])


def mlstm_chunk(carry, xs):
    C, n, m = carry
    q, k, v, li, lf = xs
    causal = jnp.tril(jnp.ones((CHUNK, CHUNK), dtype=bool))
    b = jnp.cumsum(lf, axis=-1)
    D = jnp.where(causal, b[..., :, None] - b[..., None, :] + li[..., None, :], -jnp.inf)
    inter = b + m[..., None]
    m_t = jnp.maximum(inter, jnp.max(D, axis=-1))
    P = jnp.exp(D - m_t[..., None]) * jnp.einsum('bhtk,bhsk->bhts', q, k)
    w_inter = jnp.exp(inter - m_t)
    num = jnp.einsum('bhts,bhsv->bhtv', P, v) + w_inter[..., None] * jnp.einsum('bhtk,bhkv->bhtv', q, C)
    den = jnp.sum(P, axis=-1) + w_inter * jnp.einsum('bhtk,bhk->bht', q, n)
    h = num / jnp.maximum(jnp.abs(den), jnp.exp(-m_t))[..., None]
    bL = b[..., -1]
    g_s = bL[..., None] - b + li
    m_new = jnp.maximum(bL + m, jnp.max(g_s, axis=-1))
    w_s = jnp.exp(g_s - m_new[..., None])
    w_old = jnp.exp(bL + m - m_new)
    C = w_old[..., None, None] * C + jnp.einsum('bhsk,bhsv->bhkv', k * w_s[..., None], v)
    n = w_old[..., None] * n + jnp.einsum('bhs,bhsk->bhk', w_s, k)
    return (C, n, m_new), h


def mlstm(q, k, v, li, lf):
    B_ = q.shape[0]
    init = (jnp.zeros((B_, N_HEADS, HEAD_DIM, HEAD_DIM), jnp.float32),
            jnp.zeros((B_, N_HEADS, HEAD_DIM), jnp.float32),
            jnp.zeros((B_, N_HEADS), jnp.float32))
    xs = (to_chunks(q * HEAD_DIM ** -0.5), to_chunks(k), to_chunks(v), scalar_chunks(li), scalar_chunks(lf))
    _, hs = lax.scan(mlstm_chunk, init, xs)
    return from_chunks(hs)


def gla_chunk(S, xs):
    q, k, v, g = xs
    causal = jnp.tril(jnp.ones((CHUNK, CHUNK), dtype=bool))
    b = jnp.cumsum(g, axis=2)
    diff = b[:, :, :, None, :] - b[:, :, None, :, :]
    decay = jnp.exp(jnp.where(causal[:, :, None], diff, -jnp.inf))
    A = jnp.einsum('bhtsk,bhsk->bhts', q[:, :, :, None, :] * decay, k)
    o = jnp.einsum('bhts,bhsv->bhtv', A, v) + jnp.einsum('bhtk,bhkv->bhtv', q * jnp.exp(b), S)
    bL = b[:, :, -1:, :]
    S = jnp.exp(bL[:, :, 0, :])[..., None] * S + jnp.einsum('bhsk,bhsv->bhkv', k * jnp.exp(bL - b), v)
    return S, o


def gla(q, k, v, g):
    B_ = q.shape[0]
    init = jnp.zeros((B_, N_HEADS, HEAD_DIM, HEAD_DIM), jnp.float32)
    xs = (to_chunks(q * HEAD_DIM ** -0.5), to_chunks(k), to_chunks(v), to_chunks(g))
    _, os_ = lax.scan(gla_chunk, init, xs)
    return from_chunks(os_)


def gdn_chunk(S, xs):
    q, k, v, g, beta = xs
    causal = jnp.tril(jnp.ones((CHUNK, CHUNK), dtype=bool))
    strict = jnp.tril(jnp.ones((CHUNK, CHUNK), dtype=bool), -1)
    b = jnp.cumsum(g, axis=-1)
    decay = jnp.exp(jnp.where(causal, b[..., :, None] - b[..., None, :], -jnp.inf))
    kk = jnp.einsum('bhtk,bhsk->bhts', k, k)
    A = jnp.where(strict, beta[..., None] * kk * decay, 0.0)
    eye = jnp.eye(CHUNK, dtype=A.dtype)
    rhs = jnp.concatenate([beta[..., None] * k * jnp.exp(b)[..., None], beta[..., None] * v], axis=-1)
    sol = lax.linalg.triangular_solve(eye + A, rhs, left_side=True, lower=True, unit_diagonal=True)
    w, u = sol[..., :HEAD_DIM], sol[..., HEAD_DIM:]
    v_new = u - jnp.einsum('bhtk,bhkv->bhtv', w, S)
    qk = jnp.einsum('bhtk,bhsk->bhts', q, k) * decay
    o = jnp.einsum('bhtk,bhkv->bhtv', q * jnp.exp(b)[..., None], S) + jnp.einsum('bhts,bhsv->bhtv', qk, v_new)
    bL = b[..., -1]
    S = jnp.exp(bL)[..., None, None] * S + jnp.einsum('bhsk,bhsv->bhkv', k * jnp.exp(bL[..., None] - b)[..., None], v_new)
    return S, o


def gated_deltanet(q, k, v, g, beta):
    B_ = q.shape[0]
    init = jnp.zeros((B_, N_HEADS, HEAD_DIM, HEAD_DIM), jnp.float32)
    xs = (to_chunks(q), to_chunks(k), to_chunks(v), scalar_chunks(g), scalar_chunks(beta))
    _, os_ = lax.scan(gdn_chunk, init, xs)
    return from_chunks(os_)


def mixer_block(u, w_in, b_if, w_gla_lr, b_gla, conv_w, a_log, dt_bias, g_hm, g_hl, g_hd, w_up, w_out):
    f32 = jnp.float32
    proj = u @ w_in
    idx = [int(i) for i in np.cumsum(_in_sizes())[:-1]]
    (mq, mk, mv, mo, mi, mf, lq, lk, lv, lr, llr, dq, dk, dv, dz, dbeta, da, gates) = jnp.split(proj, idx, axis=-1)
    B_, S_, _ = u.shape
    b_if = b_if.astype(f32)
    li = mi.astype(f32) + b_if[:N_HEADS]
    lf = jax.nn.log_sigmoid(mf.astype(f32) + b_if[N_HEADS:])
    h_m = mlstm(mq.astype(f32), mk.astype(f32), mv.astype(f32), li, lf)
    h_m = head_rmsnorm(jax.nn.sigmoid(mo.astype(f32)) * h_m, g_hm)
    lg = jax.nn.log_sigmoid(llr.astype(f32) @ w_gla_lr.astype(f32) + b_gla.astype(f32)) / GLA_TAU
    h_l = gla(lq.astype(f32), lk.astype(f32), lv.astype(f32), lg)
    h_l = head_rmsnorm(h_l, g_hl) * jax.nn.silu(lr.astype(f32))
    qkv = jax.nn.silu(causal_conv(jnp.concatenate([dq, dk, dv], axis=-1).astype(f32), conv_w.astype(f32)))
    gq, gk, gv = jnp.split(qkv, 3, axis=-1)
    gq = l2norm_heads(gq) * HEAD_DIM ** -0.5
    gk = l2norm_heads(gk)
    g_dec = -jnp.exp(a_log.astype(f32)) * jax.nn.softplus(da.astype(f32) + dt_bias.astype(f32))
    beta = jax.nn.sigmoid(dbeta.astype(f32))
    h_d = gated_deltanet(gq, gk, gv, g_dec, beta)
    h_d = head_rmsnorm(h_d, g_hd) * jax.nn.silu(dz.astype(f32))
    branches = jnp.stack([h_m, h_l, h_d], axis=2).astype(u.dtype)
    up = jnp.einsum('bsnc,ncd->bsnd', branches, w_up)
    gate = jax.nn.sigmoid(gates.reshape(B_, S_, N_BRANCH, D_MODEL))
    return jnp.sum(gate * up, axis=2) @ w_out


def setup_inputs(seed: int = 0) -> dict:
    key = jax.random.key(seed)
    ks = jax.random.split(key, 20)
    n_in = sum(_in_sizes())

    def nrm(k, shape, scale):
        return jax.random.normal(k, shape, jnp.float32) * scale

    x = nrm(ks[0], (BATCH, SEQ, D_MODEL), 1.0)
    w_in = nrm(ks[1], (DEPTH, D_MODEL, n_in), D_MODEL ** -0.5)
    f_bias = jnp.linspace(3.0, 6.0, N_HEADS, dtype=jnp.float32)
    b_if = jnp.concatenate([nrm(ks[2], (DEPTH, N_HEADS), 0.1),
                            f_bias + nrm(ks[3], (DEPTH, N_HEADS), 0.1)], axis=-1)
    w_gla_lr = nrm(ks[4], (DEPTH, GLA_RANK, MIX_W), GLA_RANK ** -0.5)
    b_gla = nrm(ks[5], (DEPTH, MIX_W), 0.1)
    conv_gdn = nrm(ks[6], (DEPTH, CONV_K, 3 * MIX_W), CONV_K ** -0.5)
    a_log = jnp.log(jax.random.uniform(ks[7], (DEPTH, N_HEADS), jnp.float32, 1.0, 16.0))
    dt = jnp.exp(jax.random.uniform(ks[8], (DEPTH, N_HEADS), jnp.float32, np.log(1e-3), np.log(1e-1)))
    dt_bias = dt + jnp.log(-jnp.expm1(-dt))
    g_norm_mix = 1.0 + nrm(ks[9], (DEPTH, D_MODEL), 0.02)
    g_norm_mlp = 1.0 + nrm(ks[10], (DEPTH, D_MODEL), 0.02)
    g_head_mlstm = 1.0 + nrm(ks[11], (DEPTH, MIX_W), 0.02)
    g_head_gla = 1.0 + nrm(ks[12], (DEPTH, MIX_W), 0.02)
    g_head_gdn = 1.0 + nrm(ks[13], (DEPTH, MIX_W), 0.02)
    w_up = nrm(ks[14], (DEPTH, N_BRANCH, MIX_W, D_MODEL), MIX_W ** -0.5)
    w_out = nrm(ks[15], (DEPTH, D_MODEL, D_MODEL), D_MODEL ** -0.5)
    w_mlp_in = nrm(ks[16], (DEPTH, D_MODEL, D_FF), D_MODEL ** -0.5)
    w_mlp_out = nrm(ks[17], (DEPTH, D_FF, D_MODEL), D_FF ** -0.5)
    g_final = 1.0 + nrm(ks[18], (D_MODEL,), 0.02)
    return {'x': x, 'w_in': w_in, 'b_if': b_if, 'w_gla_lr': w_gla_lr, 'b_gla': b_gla,
            'conv_gdn': conv_gdn, 'a_log': a_log, 'dt_bias': dt_bias,
            'g_norm_mix': g_norm_mix, 'g_norm_mlp': g_norm_mlp,
            'g_head_mlstm': g_head_mlstm, 'g_head_gla': g_head_gla, 'g_head_gdn': g_head_gdn,
            'w_up': w_up, 'w_out': w_out, 'w_mlp_in': w_mlp_in, 'w_mlp_out': w_mlp_out,
            'g_final': g_final}


def reference(x, w_in, b_if, w_gla_lr, b_gla, conv_gdn, a_log, dt_bias, g_norm_mix, g_norm_mlp,
              g_head_mlstm, g_head_gla, g_head_gdn, w_up, w_out, w_mlp_in, w_mlp_out, g_final):
    for l in range(DEPTH):
        u = rmsnorm(x, g_norm_mix[l])
        x = x + mixer_block(u, w_in[l], b_if[l], w_gla_lr[l], b_gla[l], conv_gdn[l], a_log[l], dt_bias[l],
                            g_head_mlstm[l], g_head_gla[l], g_head_gdn[l], w_up[l], w_out[l]).astype(x.dtype)
        u = rmsnorm(x, g_norm_mlp[l])
        x = x + jnp.square(jax.nn.relu(u @ w_mlp_in[l])) @ w_mlp_out[l]
    return rmsnorm(x, g_final)
```

```python
import functools

import numpy as np
import jax
import jax.numpy as jnp
from jax import lax
from jax.experimental import pallas as pl
from jax.experimental.pallas import tpu as pltpu

F32 = jnp.float32
BF16 = jnp.bfloat16

N_HEADS = 4
HEAD_DIM = 128
MIX_W = N_HEADS * HEAD_DIM
N_BRANCH = 3
GLA_RANK = 16
GLA_TAU = 16.0
CONV_K = 4
EPS = 1e-6

LANES = 128
SUBLANES = 8
CHUNK = 128
SOLVE_BASE = 16
NEG_BIG = -1e30
VMEM_LIMIT = 56 * 1024 * 1024

SM_LI, SM_LF, SM_LLR, SM_BETA, SM_DA = 0, 4, 8, 24, 28
SM_USED = 32
(PB_MQ, PB_MK, PB_MV, PB_MO, PB_LQ, PB_LK, PB_LV, PB_LR,
 PB_DQ, PB_DK, PB_DV, PB_DZ, PB_GATES) = range(13)
D_MODEL = 1024
N_BIG = 12 * MIX_W + N_BRANCH * D_MODEL


def _dot(a, b):
    return jnp.dot(a.astype(BF16), b.astype(BF16), preferred_element_type=F32)


def _dot_nt(a, b):
    return lax.dot_general(a.astype(BF16), b.astype(BF16), (((1,), (1,)), ((), ())),
                           preferred_element_type=F32)


def _dot_tn(a, b):
    return lax.dot_general(a.astype(BF16), b.astype(BF16), (((0,), (0,)), ((), ())),
                           preferred_element_type=F32)


def _dot_f32(a, b):
    return jnp.dot(a, b, preferred_element_type=F32, precision=lax.Precision.HIGHEST)


def _split_bf16(x, n):
    parts = []
    r = x
    for i in range(n):
        p = r.astype(BF16)
        parts.append(p)
        if i + 1 < n:
            r = r - p.astype(F32)
    return parts


def _sel_rows(m01, x, n):
    acc = None
    for p in _split_bf16(x, n):
        t = jnp.dot(m01, p, preferred_element_type=F32)
        acc = t if acc is None else acc + t
    return acc


def _sel_cols(x, m01, n):
    acc = None
    for p in _split_bf16(x, n):
        t = jnp.dot(p, m01, preferred_element_type=F32)
        acc = t if acc is None else acc + t
    return acc


def _log_sigmoid(x):
    return jnp.minimum(x, 0.0) - jnp.log(1.0 + jnp.exp(-jnp.abs(x)))


def _softplus(x):
    return jnp.maximum(x, 0.0) + jnp.log(1.0 + jnp.exp(-jnp.abs(x)))


def _sigmoid(x):
    return 1.0 / (1.0 + jnp.exp(-x))


def _silu(x):
    return x * _sigmoid(x)


def _head_rmsnorm(h, g_row):
    return h * lax.rsqrt(jnp.mean(h * h, axis=-1, keepdims=True) + EPS) * g_row


def _np_consts(L):
    t = np.arange(L)[:, None]
    u = np.arange(L)[None, :]
    tri = (u <= t)
    upper = (u > t)
    lv_m, lv_mask = [], []
    c = 1
    while c < L:
        base = (t // (2 * c)) * (2 * c)
        ref = base + c - 1
        second = t >= base + c
        m = np.where(second, (u > ref) & (u <= t), (u > t) & (u <= ref))
        ub = (u // (2 * c)) * (2 * c)
        mask = (base == ub) & second & (u < ub + c)
        lv_m.append(m)
        lv_mask.append(mask)
        c *= 2
    return tri, upper, np.stack(lv_m), np.stack(lv_mask)


def _full_spec(shape):
    nd = len(shape)
    return pl.BlockSpec(shape, lambda *_: (0,) * nd)


def _proj_kernel(x_ref, g_ref, wb_ref, ws_ref, wst_ref, p_ref, sm_ref, smt_ref, u_sc):
    j = pl.program_id(1)

    @pl.when(j == 0)
    def _():
        x = x_ref[...]
        u = x * lax.rsqrt(jnp.mean(x * x, axis=-1, keepdims=True) + EPS) * g_ref[...]
        ub = u.astype(BF16)
        u_sc[...] = ub
        sm_ref[...] = jnp.dot(ub, ws_ref[...], preferred_element_type=F32)
        smt = lax.dot_general(wst_ref[...], ub, (((1,), (1,)), ((), ())),
                              preferred_element_type=F32)
        for c in range(smt_ref.shape[0]):
            smt_ref[c] = smt[:, c * CHUNK:(c + 1) * CHUNK]

    p_ref[...] = jnp.dot(u_sc[...], wb_ref[...], preferred_element_type=F32)


def _proj(x2, g_row, w_big, w_sm, w_smt, tm, tn):
    T, D = x2.shape
    return pl.pallas_call(
        _proj_kernel,
        grid=(T // tm, N_BIG // tn),
        in_specs=[
            pl.BlockSpec((tm, D), lambda i, j: (i, 0)),
            pl.BlockSpec((1, D), lambda i, j: (0, 0)),
            pl.BlockSpec((D, tn), lambda i, j: (0, j)),
            pl.BlockSpec((D, LANES), lambda i, j: (0, 0)),
            pl.BlockSpec((SM_USED, D), lambda i, j: (0, 0)),
        ],
        out_specs=[
            pl.BlockSpec((tm, tn), lambda i, j: (i, j)),
            pl.BlockSpec((tm, LANES), lambda i, j: (i, 0)),
            pl.BlockSpec((tm // CHUNK, SM_USED, CHUNK), lambda i, j: (i, 0, 0)),
        ],
        out_shape=[
            jax.ShapeDtypeStruct((T, N_BIG), F32),
            jax.ShapeDtypeStruct((T, LANES), F32),
            jax.ShapeDtypeStruct((T // CHUNK, SM_USED, CHUNK), F32),
        ],
        scratch_shapes=[pltpu.VMEM((tm, D), BF16)],
        compiler_params=pltpu.CompilerParams(
            dimension_semantics=("arbitrary", "arbitrary"), vmem_limit_bytes=VMEM_LIMIT),
        name="proj",
    )(x2, g_row, w_big, w_sm, w_smt)


def _mlstm_kernel(q_ref, k_ref, v_ref, o_ref, sm_ref, smt_ref, brow_ref, bcol_ref, gh_ref,
                  tri_ref, trit_ref, out_ref, c_sc, n_sc, m_sc):
    L = CHUNK
    nchunk = q_ref.shape[0] // L

    @pl.when(pl.program_id(1) == 0)
    def _():
        c_sc[...] = jnp.zeros_like(c_sc)
        n_sc[...] = jnp.zeros_like(n_sc)
        m_sc[...] = jnp.zeros_like(m_sc)

    row_i = lax.broadcasted_iota(jnp.int32, (L, L), 0)
    col_i = lax.broadcasted_iota(jnp.int32, (L, L), 1)
    causal = col_i <= row_i
    sm_col = lax.broadcasted_iota(jnp.int32, (L, LANES), 1)
    is_lf_c = (sm_col >= SM_LF) & (sm_col < SM_LF + N_HEADS)
    smt_row = lax.broadcasted_iota(jnp.int32, (SM_USED, L), 0)
    is_lf_r = (smt_row >= SM_LF) & (smt_row < SM_LF + N_HEADS)
    scale = HEAD_DIM ** -0.5

    def chunk(ci, carry):
        r0 = pl.multiple_of(ci * L, L)
        rows = pl.ds(r0, L)
        tc = sm_ref[rows, :] + brow_ref[...]
        gc = jnp.where(is_lf_c, _log_sigmoid(tc), 0.0)
        bc = _sel_rows(tri_ref[...], gc, 3)
        tr = smt_ref[ci] + bcol_ref[...]
        gr = jnp.where(is_lf_r, _log_sigmoid(tr), 0.0)
        br = _sel_cols(gr, trit_ref[...], 3)
        for h in range(N_HEADS):
            cols = slice(h * HEAD_DIM, (h + 1) * HEAD_DIM)
            q = q_ref[rows, cols] * scale
            k = k_ref[rows, cols]
            v = v_ref[rows, cols]
            b_c = bc[:, SM_LF + h:SM_LF + h + 1]
            li_c = tc[:, SM_LI + h:SM_LI + h + 1]
            b_r = br[SM_LF + h:SM_LF + h + 1, :]
            li_r = tr[SM_LI + h:SM_LI + h + 1, :]
            m_old = m_sc[h][0:1, 0:1]
            n_old = n_sc[h][0:1, :]
            c_old = c_sc[h]

            d = jnp.where(causal, b_c - b_r + li_r, NEG_BIG)
            inter = b_c + m_old
            m_t = jnp.maximum(inter, jnp.max(d, axis=1, keepdims=True))
            p = jnp.exp(d - m_t) * _dot_nt(q, k)
            w_inter = jnp.exp(inter - m_t)
            num = _dot(p, v) + w_inter * _dot(q, c_old)
            den = (jnp.sum(p, axis=1, keepdims=True)
                   + w_inter * jnp.sum(q * n_old, axis=1, keepdims=True))
            hout = num / jnp.maximum(jnp.abs(den), jnp.exp(-m_t))

            b_last = b_c[L - 1:L, :]
            g_s = b_last - b_c + li_c
            m_new = jnp.maximum(b_last + m_old, jnp.max(g_s, axis=0, keepdims=True))
            w_s = jnp.exp(g_s - m_new)
            w_old = jnp.exp(b_last + m_old - m_new)
            kw = k * w_s
            c_sc[h] = w_old * c_old + _dot_tn(kw, v)
            n_new = w_old * n_old + jnp.sum(kw, axis=0, keepdims=True)
            n_sc[h] = jnp.broadcast_to(n_new, (SUBLANES, LANES))
            m_sc[h] = jnp.broadcast_to(m_new, (SUBLANES, LANES))

            gated = _sigmoid(o_ref[rows, cols]) * hout
            out_ref[rows, cols] = _head_rmsnorm(gated, gh_ref[:, cols]).astype(out_ref.dtype)
        return carry

    lax.fori_loop(0, nchunk, chunk, 0)


def _mixer_specs(tb, nblk, col_blocks):
    specs = [pl.BlockSpec((tb, MIX_W), functools.partial(lambda cb, b, c: (b * nblk + c, cb), cb))
             for cb in col_blocks]
    specs.append(pl.BlockSpec((tb, LANES), lambda b, c: (b * nblk + c, 0)))
    specs.append(pl.BlockSpec((tb // CHUNK, SM_USED, CHUNK), lambda b, c: (b * nblk + c, 0, 0)))
    return specs


def _mlstm(P, sm, smt, brow, bcol, gh, tri, trit, B, S, tb):
    T = B * S
    nblk = S // tb
    in_specs = _mixer_specs(tb, nblk, (PB_MQ, PB_MK, PB_MV, PB_MO)) + [
        _full_spec((1, LANES)), _full_spec((SM_USED, CHUNK)), _full_spec((1, MIX_W)),
        _full_spec((CHUNK, CHUNK)), _full_spec((CHUNK, CHUNK))]
    return pl.pallas_call(
        _mlstm_kernel,
        grid=(B, nblk),
        in_specs=in_specs,
        out_specs=pl.BlockSpec((tb, MIX_W), lambda b, c: (b * nblk + c, 0)),
        out_shape=jax.ShapeDtypeStruct((T, MIX_W), BF16),
        scratch_shapes=[pltpu.VMEM((N_HEADS, HEAD_DIM, HEAD_DIM), F32),
                        pltpu.VMEM((N_HEADS, SUBLANES, LANES), F32),
                        pltpu.VMEM((N_HEADS, SUBLANES, LANES), F32)],
        compiler_params=pltpu.CompilerParams(
            dimension_semantics=("arbitrary", "arbitrary"), vmem_limit_bytes=VMEM_LIMIT),
        name="mlstm",
    )(P, P, P, P, sm, smt, brow, bcol, gh, tri, trit)


def _gla_kernel(q_ref, k_ref, v_ref, r_ref, sm_ref, wlr_ref, bg_ref, gh_ref,
                tri_ref, upper_ref, lvm_ref, lvmask_ref, out_ref, st_sc):
    L = CHUNK
    nchunk = q_ref.shape[0] // L
    nlev = lvm_ref.shape[0]

    @pl.when(pl.program_id(1) == 0)
    def _():
        st_sc[...] = jnp.zeros_like(st_sc)

    scale = HEAD_DIM ** -0.5

    def chunk(ci, carry):
        r0 = pl.multiple_of(ci * L, L)
        rows = pl.ds(r0, L)
        gpre = _dot(sm_ref[rows, :], wlr_ref[...]) + bg_ref[...]
        gneg_all = _log_sigmoid(gpre) * (-1.0 / GLA_TAU)
        for h in range(N_HEADS):
            cols = slice(h * HEAD_DIM, (h + 1) * HEAD_DIM)
            q = q_ref[rows, cols] * scale
            k = k_ref[rows, cols]
            v = v_ref[rows, cols]
            hi, lo = _split_bf16(gneg_all[:, cols], 2)
            gcat = jnp.concatenate([hi, lo], axis=1)

            def seg(m01):
                e2 = jnp.dot(m01, gcat, preferred_element_type=F32)
                return e2[:, :HEAD_DIM] + e2[:, HEAD_DIM:]

            a = jnp.zeros((L, L), F32)
            for lv in range(nlev):
                ex = jnp.exp(-seg(lvm_ref[lv]))
                a = a + lvmask_ref[lv] * _dot_nt(q * ex, k * ex)
            bsum = seg(tri_ref[...])
            bup = seg(upper_ref[...])
            st_old = st_sc[h]
            dqk = jnp.sum(q * k, axis=1, keepdims=True)
            o = _dot(a, v) + dqk * v + _dot_nt(q * jnp.exp(-bsum), st_old)
            st_sc[h] = jnp.exp(-bsum[L - 1:L, :]) * st_old + _dot_tn(v, k * jnp.exp(-bup))
            hn = _head_rmsnorm(o, gh_ref[:, cols]) * _silu(r_ref[rows, cols])
            out_ref[rows, cols] = hn.astype(out_ref.dtype)
        return carry

    lax.fori_loop(0, nchunk, chunk, 0)


def _gla(P, sm, wlr, bg, gh, tri, upper, lvm, lvmask, B, S, tb):
    T = B * S
    nblk = S // tb
    nlev = lvm.shape[0]
    in_specs = _mixer_specs(tb, nblk, (PB_LQ, PB_LK, PB_LV, PB_LR))[:-1] + [
        _full_spec((LANES, MIX_W)), _full_spec((1, MIX_W)), _full_spec((1, MIX_W)),
        _full_spec((CHUNK, CHUNK)), _full_spec((CHUNK, CHUNK)),
        _full_spec((nlev, CHUNK, CHUNK)), _full_spec((nlev, CHUNK, CHUNK))]
    return pl.pallas_call(
        _gla_kernel,
        grid=(B, nblk),
        in_specs=in_specs,
        out_specs=pl.BlockSpec((tb, MIX_W), lambda b, c: (b * nblk + c, 0)),
        out_shape=jax.ShapeDtypeStruct((T, MIX_W), BF16),
        scratch_shapes=[pltpu.VMEM((N_HEADS, HEAD_DIM, HEAD_DIM), F32)],
        compiler_params=pltpu.CompilerParams(
            dimension_semantics=("arbitrary", "arbitrary"), vmem_limit_bytes=VMEM_LIMIT),
        name="gla",
    )(P, P, P, P, sm, wlr, bg, gh, tri, upper, lvm, lvmask)


def _unit_lower_inverse(a, eye, blk_masks):
    a0 = a * blk_masks[0]
    x = eye - a0
    pw = _dot_f32(a0, a0)
    x = x + _dot_f32(x, pw)
    span = 4
    while span < SOLVE_BASE:
        pw = _dot_f32(pw, pw)
        x = x + _dot_f32(x, pw)
        span *= 2
    for m in blk_masks[1:]:
        x = x - _dot_f32(x, _dot_f32(a * m, x))
    return x


def _gdn_kernel(q_ref, k_ref, v_ref, z_ref, sm_ref, smt_ref, cw_ref, arow_ref, acol_ref,
                drow_ref, dcol_ref, gh_ref, tri_ref, trit_ref, eye_ref, blk_ref, out_ref,
                s_sc, tail_sc, qc_sc, kc_sc, vc_sc):
    L = CHUNK
    tb = q_ref.shape[0]
    nchunk = tb // L
    nblk_masks = blk_ref.shape[0]

    @pl.when(pl.program_id(1) == 0)
    def _():
        s_sc[...] = jnp.zeros_like(s_sc)
        tail_sc[...] = jnp.zeros_like(tail_sc)

    def conv_silu(x_ref, idx):
        w = cw_ref[:, idx * MIX_W:(idx + 1) * MIX_W]
        x = x_ref[...]

        def taps(z):
            y = z * w[CONV_K - 1:CONV_K, :]
            for d in range(1, CONV_K):
                y = y + pltpu.roll(z, d, 0) * w[CONV_K - 1 - d:CONV_K - d, :]
            return y

        head = jnp.concatenate([tail_sc[idx], x[0:SUBLANES, :]], axis=0)
        y_head = taps(head)[SUBLANES:2 * SUBLANES, :]
        tail_sc[idx] = x[tb - SUBLANES:tb, :]
        return _silu(taps(x)), _silu(y_head)

    def l2n(y, mult):
        outs = []
        for h in range(N_HEADS):
            yh = y[:, h * HEAD_DIM:(h + 1) * HEAD_DIM]
            outs.append(yh * (lax.rsqrt(jnp.sum(yh * yh, axis=-1, keepdims=True) + EPS) * mult))
        return jnp.concatenate(outs, axis=1)

    for idx, (src, dst, mult) in enumerate(((q_ref, qc_sc, HEAD_DIM ** -0.5), (k_ref, kc_sc, 1.0),
                                            (v_ref, vc_sc, None))):
        y, y_head = conv_silu(src, idx)
        if mult is not None:
            y, y_head = l2n(y, mult), l2n(y_head, mult)
        dst[...] = y
        dst[0:SUBLANES, :] = y_head

    row_i = lax.broadcasted_iota(jnp.int32, (L, L), 0)
    col_i = lax.broadcasted_iota(jnp.int32, (L, L), 1)
    causal = col_i <= row_i
    strict = col_i < row_i
    sm_col = lax.broadcasted_iota(jnp.int32, (L, LANES), 1)
    is_da_c = (sm_col >= SM_DA) & (sm_col < SM_DA + N_HEADS)
    smt_row = lax.broadcasted_iota(jnp.int32, (SM_USED, L), 0)
    is_da_r = (smt_row >= SM_DA) & (smt_row < SM_DA + N_HEADS)
    blk_masks = [blk_ref[i] for i in range(nblk_masks)]

    def chunk(ci, carry):
        r0 = pl.multiple_of(ci * L, L)
        rows = pl.ds(r0, L)
        tc = sm_ref[rows, :]
        gc = jnp.where(is_da_c, -jnp.exp(arow_ref[...]) * _softplus(tc + drow_ref[...]), 0.0)
        bc = _sel_rows(tri_ref[...], gc, 3)
        beta_all = _sigmoid(tc)
        tr = smt_ref[ci]
        gr = jnp.where(is_da_r, -jnp.exp(acol_ref[...]) * _softplus(tr + dcol_ref[...]), 0.0)
        br = _sel_cols(gr, trit_ref[...], 3)
        for h in range(N_HEADS):
            cols = slice(h * HEAD_DIM, (h + 1) * HEAD_DIM)
            q = qc_sc[rows, cols]
            k = kc_sc[rows, cols]
            v = vc_sc[rows, cols]
            b_c = bc[:, SM_DA + h:SM_DA + h + 1]
            b_r = br[SM_DA + h:SM_DA + h + 1, :]
            beta = beta_all[:, SM_BETA + h:SM_BETA + h + 1]
            s_old = s_sc[h]

            decay = jnp.where(causal, jnp.exp(jnp.minimum(b_c - b_r, 0.0)), 0.0)
            a = jnp.where(strict, beta * _dot_nt(k, k) * decay, 0.0)
            tinv = _unit_lower_inverse(a, eye_ref[...], blk_masks)
            eb = jnp.exp(b_c)
            w = _dot(tinv, (beta * eb) * k)
            u = _dot(tinv, beta * v)
            v_new = u - _dot(w, s_old)
            qk = _dot_nt(q, k) * decay
            o = _dot(q * eb, s_old) + _dot(qk, v_new)
            b_last = b_c[L - 1:L, :]
            s_sc[h] = jnp.exp(b_last) * s_old + _dot_tn(k * jnp.exp(b_last - b_c), v_new)
            hn = _head_rmsnorm(o, gh_ref[:, cols]) * _silu(z_ref[rows, cols])
            out_ref[rows, cols] = hn.astype(out_ref.dtype)
        return carry

    lax.fori_loop(0, nchunk, chunk, 0)


def _gdn(P, sm, smt, cw, arow, acol, drow, dcol, gh, tri, trit, eye, blk, B, S, tb):
    T = B * S
    nblk = S // tb
    in_specs = _mixer_specs(tb, nblk, (PB_DQ, PB_DK, PB_DV, PB_DZ)) + [
        _full_spec((CONV_K, 3 * MIX_W)),
        _full_spec((1, LANES)), _full_spec((SM_USED, CHUNK)),
        _full_spec((1, LANES)), _full_spec((SM_USED, CHUNK)),
        _full_spec((1, MIX_W)),
        _full_spec((CHUNK, CHUNK)), _full_spec((CHUNK, CHUNK)), _full_spec((CHUNK, CHUNK)),
        _full_spec(blk.shape)]
    return pl.pallas_call(
        _gdn_kernel,
        grid=(B, nblk),
        in_specs=in_specs,
        out_specs=pl.BlockSpec((tb, MIX_W), lambda b, c: (b * nblk + c, 0)),
        out_shape=jax.ShapeDtypeStruct((T, MIX_W), BF16),
        scratch_shapes=[pltpu.VMEM((N_HEADS, HEAD_DIM, HEAD_DIM), F32),
                        pltpu.VMEM((3, SUBLANES, MIX_W), F32),
                        pltpu.VMEM((tb, MIX_W), F32),
                        pltpu.VMEM((tb, MIX_W), F32),
                        pltpu.VMEM((tb, MIX_W), F32)],
        compiler_params=pltpu.CompilerParams(
            dimension_semantics=("arbitrary", "arbitrary"), vmem_limit_bytes=VMEM_LIMIT),
        name="gdn",
    )(P, P, P, P, sm, smt, cw, arow, acol, drow, dcol, gh, tri, trit, eye, blk)


def _merge_kernel(x_ref, hm_ref, hl_ref, hd_ref, gates_ref, wup_ref, wout_ref, out_ref):
    d = x_ref.shape[1]
    acc = None
    for n, h_ref in enumerate((hm_ref, hl_ref, hd_ref)):
        up = jnp.dot(h_ref[...], wup_ref[n], preferred_element_type=F32)
        t = _sigmoid(gates_ref[:, n * d:(n + 1) * d]) * up
        acc = t if acc is None else acc + t
    out_ref[...] = x_ref[...] + jnp.dot(acc.astype(BF16), wout_ref[...], preferred_element_type=F32)


def _merge(x2, hm, hl, hd, P, wup, wout, tm):
    T, D = x2.shape
    gate_blk = (PB_GATES * MIX_W) // (N_BRANCH * D)
    row = lambda i: (i, 0)
    return pl.pallas_call(
        _merge_kernel,
        grid=(T // tm,),
        in_specs=[pl.BlockSpec((tm, D), row), pl.BlockSpec((tm, MIX_W), row),
                  pl.BlockSpec((tm, MIX_W), row), pl.BlockSpec((tm, MIX_W), row),
                  pl.BlockSpec((tm, N_BRANCH * D), lambda i: (i, gate_blk)),
                  _full_spec((N_BRANCH, MIX_W, D)), _full_spec((D, D))],
        out_specs=pl.BlockSpec((tm, D), row),
        out_shape=jax.ShapeDtypeStruct((T, D), F32),
        compiler_params=pltpu.CompilerParams(
            dimension_semantics=("arbitrary",), vmem_limit_bytes=VMEM_LIMIT),
        name="merge",
    )(x2, hm, hl, hd, P, wup, wout)


def _mlp_kernel(x_ref, g_ref, w1_ref, w2_ref, out_ref, u_sc, acc_sc):
    f = pl.program_id(1)

    @pl.when(f == 0)
    def _():
        x = x_ref[...]
        u = x * lax.rsqrt(jnp.mean(x * x, axis=-1, keepdims=True) + EPS) * g_ref[...]
        u_sc[...] = u.astype(BF16)
        acc_sc[...] = jnp.zeros_like(acc_sc)

    hmid = jnp.maximum(jnp.dot(u_sc[...], w1_ref[...], preferred_element_type=F32), 0.0)
    acc_sc[...] += jnp.dot((hmid * hmid).astype(BF16), w2_ref[...], preferred_element_type=F32)

    @pl.when(f == pl.num_programs(1) - 1)
    def _():
        out_ref[...] = x_ref[...] + acc_sc[...]


def _mlp(x2, g_row, w1, w2, tm, tf):
    T, D = x2.shape
    F = w1.shape[1]
    return pl.pallas_call(
        _mlp_kernel,
        grid=(T // tm, F // tf),
        in_specs=[pl.BlockSpec((tm, D), lambda i, f: (i, 0)),
                  pl.BlockSpec((1, D), lambda i, f: (0, 0)),
                  pl.BlockSpec((D, tf), lambda i, f: (0, f)),
                  pl.BlockSpec((tf, D), lambda i, f: (f, 0))],
        out_specs=pl.BlockSpec((tm, D), lambda i, f: (i, 0)),
        out_shape=jax.ShapeDtypeStruct((T, D), F32),
        scratch_shapes=[pltpu.VMEM((tm, D), BF16), pltpu.VMEM((tm, D), F32)],
        compiler_params=pltpu.CompilerParams(
            dimension_semantics=("arbitrary", "arbitrary"), vmem_limit_bytes=VMEM_LIMIT),
        name="mlp",
    )(x2, g_row, w1, w2)


def _final_norm_kernel(x_ref, g_ref, out_ref):
    x = x_ref[...]
    out_ref[...] = x * lax.rsqrt(jnp.mean(x * x, axis=-1, keepdims=True) + EPS) * g_ref[...]


def _final_norm(x2, g_row, tm):
    T, D = x2.shape
    return pl.pallas_call(
        _final_norm_kernel,
        grid=(T // tm,),
        in_specs=[pl.BlockSpec((tm, D), lambda i: (i, 0)), pl.BlockSpec((1, D), lambda i: (0, 0))],
        out_specs=pl.BlockSpec((tm, D), lambda i: (i, 0)),
        out_shape=jax.ShapeDtypeStruct((T, D), F32),
        compiler_params=pltpu.CompilerParams(dimension_semantics=("arbitrary",)),
        name="final_norm",
    )(x2, g_row)


def _tile(n, want):
    t = min(n, want)
    while n % t:
        t //= 2
    return t


def _pad_row(vals, offset, width=LANES):
    return jnp.zeros((1, width), F32).at[0, offset:offset + vals.shape[0]].set(vals.astype(F32))


def kernel(x, w_in, b_if, w_gla_lr, b_gla, conv_gdn, a_log, dt_bias, g_norm_mix, g_norm_mlp,
           g_head_mlstm, g_head_gla, g_head_gdn, w_up, w_out, w_mlp_in, w_mlp_out, g_final):
    B, S, D = x.shape
    depth = w_in.shape[0]
    T = B * S
    H, W = N_HEADS, MIX_W
    assert S % CHUNK == 0 and D == D_MODEL

    tri_np, upper_np, lvm_np, lvmask_np = _np_consts(CHUNK)
    tri = jnp.asarray(tri_np, BF16)
    trit = jnp.asarray(tri_np.T, BF16)
    upper = jnp.asarray(upper_np, BF16)
    lvm = jnp.asarray(lvm_np, BF16)
    lvmask = jnp.asarray(lvmask_np, F32)
    eye = jnp.asarray(np.eye(CHUNK), F32)
    ti = np.arange(CHUNK)[:, None]
    ui = np.arange(CHUNK)[None, :]
    blk_list = [(ti // SOLVE_BASE) == (ui // SOLVE_BASE)]
    c = SOLVE_BASE
    while c < CHUNK:
        blk_list.append(((ti // (2 * c)) == (ui // (2 * c))) & ((ti // c) != (ui // c)))
        c *= 2
    blk = jnp.asarray(np.stack(blk_list), F32)

    sizes = [W, W, W, W, H, H, W, W, W, W, GLA_RANK, W, W, W, W, H, H, N_BRANCH * D]
    offs = np.concatenate([[0], np.cumsum(sizes)])
    big_ids = [0, 1, 2, 3, 6, 7, 8, 9, 11, 12, 13, 14, 17]
    small_ids = [4, 5, 10, 15, 16]

    tm_proj = _tile(T, 1024)
    tn_proj = 1536
    tb = _tile(S, 512)
    tm_merge = _tile(T, 512)
    tm_mlp = _tile(T, 1024)
    tf_mlp = _tile(w_mlp_in.shape[2], 1024)

    x2 = x.reshape(T, D)
    for l in range(depth):
        wl = w_in[l]
        w_big = jnp.concatenate([wl[:, offs[i]:offs[i + 1]] for i in big_ids], axis=1).astype(BF16)
        w_small = jnp.concatenate([wl[:, offs[i]:offs[i + 1]] for i in small_ids], axis=1)
        w_sm = jnp.pad(w_small, ((0, 0), (0, LANES - SM_USED))).astype(BF16)
        w_smt = w_small.T.astype(BF16)
        P, sm, smt = _proj(x2, g_norm_mix[l].reshape(1, D), w_big, w_sm, w_smt, tm_proj, tn_proj)

        bif_row = _pad_row(b_if[l], SM_LI)
        bif_col = jnp.broadcast_to(bif_row[0, :SM_USED, None], (SM_USED, CHUNK))
        hm = _mlstm(P, sm, smt, bif_row, bif_col, g_head_mlstm[l].reshape(1, W), tri, trit, B, S, tb)

        wlr = jnp.zeros((LANES, W), F32).at[SM_LLR:SM_LLR + GLA_RANK].set(w_gla_lr[l]).astype(BF16)
        hl = _gla(P, sm, wlr, b_gla[l].reshape(1, W), g_head_gla[l].reshape(1, W),
                  tri, upper, lvm, lvmask, B, S, tb)

        arow = _pad_row(a_log[l], SM_DA)
        drow = _pad_row(dt_bias[l], SM_DA)
        acol = jnp.broadcast_to(arow[0, :SM_USED, None], (SM_USED, CHUNK))
        dcol = jnp.broadcast_to(drow[0, :SM_USED, None], (SM_USED, CHUNK))
        hd = _gdn(P, sm, smt, conv_gdn[l], arow, acol, drow, dcol, g_head_gdn[l].reshape(1, W),
                  tri, trit, eye, blk, B, S, tb)

        x2 = _merge(x2, hm, hl, hd, P, w_up[l].astype(BF16), w_out[l].astype(BF16), tm_merge)
        x2 = _mlp(x2, g_norm_mlp[l].reshape(1, D), w_mlp_in[l].astype(BF16),
                  w_mlp_out[l].astype(BF16), tm_mlp, tf_mlp)
    out = _final_norm(x2, g_final.reshape(1, D), _tile(T, 1024))
    return out.reshape(B, S, D)
```

```python
import functools

import numpy as np
import jax
import jax.numpy as jnp
from jax import lax
from jax.experimental import pallas as pl
from jax.experimental.pallas import tpu as pltpu

F32 = jnp.float32
BF16 = jnp.bfloat16

N_HEADS = 4
HEAD_DIM = 128
MIX_W = N_HEADS * HEAD_DIM
N_BRANCH = 3
GLA_RANK = 16
GLA_TAU = 16.0
CONV_K = 4
EPS = 1e-6

LANES = 128
SUBLANES = 8
CHUNK = 128
SOLVE_BASE = 16
NEG_BIG = -1e30
VMEM_LIMIT = 56 * 1024 * 1024

SM_LI, SM_LF, SM_LLR, SM_BETA, SM_DA = 0, 4, 8, 24, 28
SM_USED = 32
(PB_MQ, PB_MK, PB_MV, PB_MO, PB_LQ, PB_LK, PB_LV, PB_LR,
 PB_DQ, PB_DK, PB_DV, PB_DZ, PB_GATES) = range(13)
D_MODEL = 1024
N_BIG = 12 * MIX_W + N_BRANCH * D_MODEL


def _dot(a, b):
    return jnp.dot(a.astype(BF16), b.astype(BF16), preferred_element_type=F32)


def _dot_nt(a, b):
    return lax.dot_general(a.astype(BF16), b.astype(BF16), (((1,), (1,)), ((), ())),
                           preferred_element_type=F32)


def _dot_tn(a, b):
    return lax.dot_general(a.astype(BF16), b.astype(BF16), (((0,), (0,)), ((), ())),
                           preferred_element_type=F32)


def _split_bf16(x, n):
    parts = []
    r = x
    for i in range(n):
        p = r.astype(BF16)
        parts.append(p)
        if i + 1 < n:
            r = r - p.astype(F32)
    return parts


def _sel_rows(m01, x, n):
    acc = None
    for p in _split_bf16(x, n):
        t = jnp.dot(m01, p, preferred_element_type=F32)
        acc = t if acc is None else acc + t
    return acc


def _sel_cols(x, m01, n):
    acc = None
    for p in _split_bf16(x, n):
        t = jnp.dot(p, m01, preferred_element_type=F32)
        acc = t if acc is None else acc + t
    return acc


def _log_sigmoid(x):
    return jnp.minimum(x, 0.0) - jnp.log(1.0 + jnp.exp(-jnp.abs(x)))


def _softplus(x):
    return jnp.maximum(x, 0.0) + jnp.log(1.0 + jnp.exp(-jnp.abs(x)))


def _sigmoid(x):
    return 1.0 / (1.0 + jnp.exp(-x))


def _silu(x):
    return x * _sigmoid(x)


def _head_rmsnorm(h, g_row):
    return h * lax.rsqrt(jnp.mean(h * h, axis=-1, keepdims=True) + EPS) * g_row


def _round_robin(gens):
    live = list(gens)
    while live:
        nxt = []
        for g in live:
            try:
                next(g)
                nxt.append(g)
            except StopIteration:
                pass
        live = nxt


def _np_consts(L):
    t = np.arange(L)[:, None]
    u = np.arange(L)[None, :]
    tri = (u <= t)
    lv_m, lv_mask = [], []
    c = 1
    while c < L:
        base = (t // (2 * c)) * (2 * c)
        ref = base + c - 1
        second = t >= base + c
        m = np.where(second, (u > ref) & (u <= t), (u > t) & (u <= ref))
        ub = (u // (2 * c)) * (2 * c)
        mask = (base == ub) & second & (u < ub + c)
        lv_m.append(m)
        lv_mask.append(mask)
        c *= 2
    return tri, np.stack(lv_m), np.stack(lv_mask)


def _full_spec(shape):
    nd = len(shape)
    return pl.BlockSpec(shape, lambda *_: (0,) * nd)


def _proj_kernel(x_ref, g_ref, wb_ref, ws_ref, wst_ref, p_ref, sm_ref, smt_ref, u_sc):
    j = pl.program_id(1)

    @pl.when(j == 0)
    def _():
        x = x_ref[...]
        u = x * lax.rsqrt(jnp.mean(x * x, axis=-1, keepdims=True) + EPS) * g_ref[...]
        ub = u.astype(BF16)
        u_sc[...] = ub
        sm_ref[...] = jnp.dot(ub, ws_ref[...], preferred_element_type=F32)
        smt = lax.dot_general(wst_ref[...], ub, (((1,), (1,)), ((), ())),
                              preferred_element_type=F32)
        for c in range(smt_ref.shape[0]):
            smt_ref[c] = smt[:, c * CHUNK:(c + 1) * CHUNK]

    p_ref[...] = jnp.dot(u_sc[...], wb_ref[...], preferred_element_type=F32)


def _proj(x2, g_row, w_big, w_sm, w_smt, tm, tn):
    T, D = x2.shape
    return pl.pallas_call(
        _proj_kernel,
        grid=(T // tm, N_BIG // tn),
        in_specs=[
            pl.BlockSpec((tm, D), lambda i, j: (i, 0)),
            pl.BlockSpec((1, D), lambda i, j: (0, 0)),
            pl.BlockSpec((D, tn), lambda i, j: (0, j)),
            pl.BlockSpec((D, LANES), lambda i, j: (0, 0)),
            pl.BlockSpec((SM_USED, D), lambda i, j: (0, 0)),
        ],
        out_specs=[
            pl.BlockSpec((tm, tn), lambda i, j: (i, j)),
            pl.BlockSpec((tm, LANES), lambda i, j: (i, 0)),
            pl.BlockSpec((tm // CHUNK, SM_USED, CHUNK), lambda i, j: (i, 0, 0)),
        ],
        out_shape=[
            jax.ShapeDtypeStruct((T, N_BIG), F32),
            jax.ShapeDtypeStruct((T, LANES), F32),
            jax.ShapeDtypeStruct((T // CHUNK, SM_USED, CHUNK), F32),
        ],
        scratch_shapes=[pltpu.VMEM((tm, D), BF16)],
        compiler_params=pltpu.CompilerParams(
            dimension_semantics=("arbitrary", "arbitrary"), vmem_limit_bytes=VMEM_LIMIT),
        name="proj",
    )(x2, g_row, w_big, w_sm, w_smt)


def _mlstm_kernel(q_ref, k_ref, v_ref, o_ref, sm_ref, smt_ref, brow_ref, bcol_ref, gh_ref,
                  tri_ref, trit_ref, out_ref, c_sc, n_sc, m_sc):
    L = CHUNK
    nchunk = q_ref.shape[0] // L

    @pl.when(pl.program_id(1) == 0)
    def _():
        c_sc[...] = jnp.zeros_like(c_sc)
        n_sc[...] = jnp.zeros_like(n_sc)
        m_sc[...] = jnp.zeros_like(m_sc)

    row_i = lax.broadcasted_iota(jnp.int32, (L, L), 0)
    col_i = lax.broadcasted_iota(jnp.int32, (L, L), 1)
    causal = col_i <= row_i
    sm_col = lax.broadcasted_iota(jnp.int32, (L, LANES), 1)
    is_lf_c = (sm_col >= SM_LF) & (sm_col < SM_LF + N_HEADS)
    smt_row = lax.broadcasted_iota(jnp.int32, (SM_USED, L), 0)
    is_lf_r = (smt_row >= SM_LF) & (smt_row < SM_LF + N_HEADS)
    scale = HEAD_DIM ** -0.5

    def chunk(ci, carry):
        r0 = pl.multiple_of(ci * L, L)
        rows = pl.ds(r0, L)
        tc = sm_ref[rows, :] + brow_ref[...]
        gc = jnp.where(is_lf_c, _log_sigmoid(tc), 0.0)
        bc = _sel_rows(tri_ref[...], gc, 3)
        tr = smt_ref[ci] + bcol_ref[...]
        gr = jnp.where(is_lf_r, _log_sigmoid(tr), 0.0)
        br = _sel_cols(gr, trit_ref[...], 3)
        def head(h):
            cols = slice(h * HEAD_DIM, (h + 1) * HEAD_DIM)
            q = q_ref[rows, cols] * scale
            k = k_ref[rows, cols]
            v = v_ref[rows, cols]
            b_c = bc[:, SM_LF + h:SM_LF + h + 1]
            li_c = tc[:, SM_LI + h:SM_LI + h + 1]
            b_r = br[SM_LF + h:SM_LF + h + 1, :]
            li_r = tr[SM_LI + h:SM_LI + h + 1, :]
            m_old = m_sc[h][0:1, 0:1]
            n_old = n_sc[h][0:1, :]
            c_old = c_sc[h]

            s_qk = _dot_nt(q, k)
            q_c = _dot(q, c_old)
            yield
            d = jnp.where(causal, b_c - b_r + li_r, NEG_BIG)
            inter = b_c + m_old
            m_t = jnp.maximum(inter, jnp.max(d, axis=1, keepdims=True))
            p = jnp.exp(d - m_t) * s_qk
            w_inter = jnp.exp(inter - m_t)
            b_last = b_c[L - 1:L, :]
            g_s = b_last - b_c + li_c
            m_new = jnp.maximum(b_last + m_old, jnp.max(g_s, axis=0, keepdims=True))
            w_s = jnp.exp(g_s - m_new)
            w_old = jnp.exp(b_last + m_old - m_new)
            kw = k * w_s
            pv = _dot(p, v)
            kv = _dot_tn(kw, v)
            yield
            num = pv + w_inter * q_c
            den = (jnp.sum(p, axis=1, keepdims=True)
                   + w_inter * jnp.sum(q * n_old, axis=1, keepdims=True))
            hout = num / jnp.maximum(jnp.abs(den), jnp.exp(-m_t))
            c_sc[h] = w_old * c_old + kv
            n_new = w_old * n_old + jnp.sum(kw, axis=0, keepdims=True)
            n_sc[h] = jnp.broadcast_to(n_new, (SUBLANES, LANES))
            m_sc[h] = jnp.broadcast_to(m_new, (SUBLANES, LANES))

            gated = _sigmoid(o_ref[rows, cols]) * hout
            out_ref[rows, cols] = _head_rmsnorm(gated, gh_ref[:, cols]).astype(out_ref.dtype)

        for h in range(N_HEADS):
            _round_robin([head(h)])
        return carry

    lax.fori_loop(0, nchunk, chunk, 0)


def _mixer_specs(tb, nblk, col_blocks):
    specs = [pl.BlockSpec((tb, MIX_W), functools.partial(lambda cb, b, c: (b * nblk + c, cb), cb))
             for cb in col_blocks]
    specs.append(pl.BlockSpec((tb, LANES), lambda b, c: (b * nblk + c, 0)))
    specs.append(pl.BlockSpec((tb // CHUNK, SM_USED, CHUNK), lambda b, c: (b * nblk + c, 0, 0)))
    return specs


def _mlstm(P, sm, smt, brow, bcol, gh, tri, trit, B, S, tb):
    T = B * S
    nblk = S // tb
    in_specs = _mixer_specs(tb, nblk, (PB_MQ, PB_MK, PB_MV, PB_MO)) + [
        _full_spec((1, LANES)), _full_spec((SM_USED, CHUNK)), _full_spec((1, MIX_W)),
        _full_spec((CHUNK, CHUNK)), _full_spec((CHUNK, CHUNK))]
    return pl.pallas_call(
        _mlstm_kernel,
        grid=(B, nblk),
        in_specs=in_specs,
        out_specs=pl.BlockSpec((tb, MIX_W), lambda b, c: (b * nblk + c, 0)),
        out_shape=jax.ShapeDtypeStruct((T, MIX_W), BF16),
        scratch_shapes=[pltpu.VMEM((N_HEADS, HEAD_DIM, HEAD_DIM), F32),
                        pltpu.VMEM((N_HEADS, SUBLANES, LANES), F32),
                        pltpu.VMEM((N_HEADS, SUBLANES, LANES), F32)],
        compiler_params=pltpu.CompilerParams(
            dimension_semantics=("arbitrary", "arbitrary"), vmem_limit_bytes=VMEM_LIMIT),
        name="mlstm",
    )(P, P, P, P, sm, smt, brow, bcol, gh, tri, trit)


def _gla_kernel(q_ref, k_ref, v_ref, r_ref, sm_ref, wlr_ref, bg_ref, gh_ref,
                segm_ref, lvmask_ref, out_ref, st_sc):
    L = CHUNK
    nchunk = q_ref.shape[0] // L
    nlev = lvmask_ref.shape[0]
    nlow = segm_ref.shape[0] // L - 1

    @pl.when(pl.program_id(1) == 0)
    def _():
        st_sc[...] = jnp.zeros_like(st_sc)

    scale = HEAD_DIM ** -0.5

    def chunk(ci, carry):
        r0 = pl.multiple_of(ci * L, L)
        rows = pl.ds(r0, L)
        gpre = _dot(sm_ref[rows, :], wlr_ref[...]) + bg_ref[...]
        gneg_all = _log_sigmoid(gpre) * (-1.0 / GLA_TAU)
        def head(h):
            cols = slice(h * HEAD_DIM, (h + 1) * HEAD_DIM)
            q = q_ref[rows, cols] * scale
            k = k_ref[rows, cols]
            v = v_ref[rows, cols]
            hi, lo = _split_bf16(gneg_all[:, cols], 2)
            gcat = jnp.concatenate([hi, lo], axis=1)
            e2 = jnp.dot(segm_ref[...], gcat, preferred_element_type=F32)
            yield
            e = e2[:, :HEAD_DIM] + e2[:, HEAD_DIM:]
            bsum = e[nlow * L:(nlow + 1) * L]
            bup = bsum[L - 1:L, :] - bsum
            st_old = st_sc[h]
            q_st = _dot_nt(q * jnp.exp(-bsum), st_old)
            vk = _dot_tn(v, k * jnp.exp(-bup))
            yield
            st_sc[h] = jnp.exp(-bsum[L - 1:L, :]) * st_old + vk
            a = jnp.zeros((L, L), F32)
            for lv in range(nlev):
                if lv < nlow:
                    dist = e[lv * L:(lv + 1) * L]
                else:
                    half = 1 << lv
                    b3 = bsum.reshape(L // (2 * half), 2 * half, HEAD_DIM)
                    dist = jnp.abs(b3 - b3[:, half - 1:half, :]).reshape(L, HEAD_DIM)
                ex = jnp.exp(-dist)
                a = a + lvmask_ref[lv] * _dot_nt(q * ex, k * ex)
                yield
            dqk = jnp.sum(q * k, axis=1, keepdims=True)
            o = _dot(a, v) + dqk * v + q_st
            yield
            hn = _head_rmsnorm(o, gh_ref[:, cols]) * _silu(r_ref[rows, cols])
            out_ref[rows, cols] = hn.astype(out_ref.dtype)

        _round_robin(head(h) for h in range(N_HEADS))
        return carry

    lax.fori_loop(0, nchunk, chunk, 0)


def _gla(P, sm, wlr, bg, gh, segm, lvmask, B, S, tb):
    T = B * S
    nblk = S // tb
    in_specs = _mixer_specs(tb, nblk, (PB_LQ, PB_LK, PB_LV, PB_LR))[:-1] + [
        _full_spec((LANES, MIX_W)), _full_spec((1, MIX_W)), _full_spec((1, MIX_W)),
        _full_spec(segm.shape), _full_spec(lvmask.shape)]
    return pl.pallas_call(
        _gla_kernel,
        grid=(B, nblk),
        in_specs=in_specs,
        out_specs=pl.BlockSpec((tb, MIX_W), lambda b, c: (b * nblk + c, 0)),
        out_shape=jax.ShapeDtypeStruct((T, MIX_W), BF16),
        scratch_shapes=[pltpu.VMEM((N_HEADS, HEAD_DIM, HEAD_DIM), F32)],
        compiler_params=pltpu.CompilerParams(
            dimension_semantics=("arbitrary", "arbitrary"), vmem_limit_bytes=VMEM_LIMIT),
        name="gla",
    )(P, P, P, P, sm, wlr, bg, gh, segm, lvmask)


def _unit_lower_inverse_minus_eye(a, blk_masks):
    a0 = a * blk_masks[0]
    n = -a0
    pw = _dot(a0, a0)
    yield
    span = 2
    while span < SOLVE_BASE:
        n_pw = _dot(n, pw)
        pw_next = _dot(pw, pw) if 2 * span < SOLVE_BASE else None
        yield
        n = n + pw + n_pw
        pw = pw_next
        span *= 2
    for m in blk_masks[1:]:
        c = a * m
        n_c = _dot(n, c)
        yield
        t = c + n_c
        t_n = _dot(t, n)
        yield
        n = n - t - t_n
    return n


def _gdn_kernel(q_ref, k_ref, v_ref, z_ref, sm_ref, smt_ref, cw_ref, arow_ref, acol_ref,
                drow_ref, dcol_ref, gh_ref, tri_ref, trit_ref, blk_ref, out_ref,
                s_sc, tail_sc, qc_sc, kc_sc, vc_sc):
    L = CHUNK
    tb = q_ref.shape[0]
    nchunk = tb // L
    nblk_masks = blk_ref.shape[0]

    @pl.when(pl.program_id(1) == 0)
    def _():
        s_sc[...] = jnp.zeros_like(s_sc)
        tail_sc[...] = jnp.zeros_like(tail_sc)

    def conv_silu(x_ref, idx):
        w = cw_ref[:, idx * MIX_W:(idx + 1) * MIX_W]
        x = x_ref[...]

        def taps(z):
            y = z * w[CONV_K - 1:CONV_K, :]
            for d in range(1, CONV_K):
                y = y + pltpu.roll(z, d, 0) * w[CONV_K - 1 - d:CONV_K - d, :]
            return y

        head = jnp.concatenate([tail_sc[idx], x[0:SUBLANES, :]], axis=0)
        y_head = taps(head)[SUBLANES:2 * SUBLANES, :]
        tail_sc[idx] = x[tb - SUBLANES:tb, :]
        return _silu(taps(x)), _silu(y_head)

    def l2n(y, mult):
        outs = []
        for h in range(N_HEADS):
            yh = y[:, h * HEAD_DIM:(h + 1) * HEAD_DIM]
            outs.append(yh * (lax.rsqrt(jnp.sum(yh * yh, axis=-1, keepdims=True) + EPS) * mult))
        return jnp.concatenate(outs, axis=1)

    for idx, (src, dst, mult) in enumerate(((q_ref, qc_sc, HEAD_DIM ** -0.5), (k_ref, kc_sc, 1.0),
                                            (v_ref, vc_sc, None))):
        y, y_head = conv_silu(src, idx)
        if mult is not None:
            y, y_head = l2n(y, mult), l2n(y_head, mult)
        dst[...] = y
        dst[0:SUBLANES, :] = y_head

    row_i = lax.broadcasted_iota(jnp.int32, (L, L), 0)
    col_i = lax.broadcasted_iota(jnp.int32, (L, L), 1)
    causal = col_i <= row_i
    sm_col = lax.broadcasted_iota(jnp.int32, (L, LANES), 1)
    is_da_c = (sm_col >= SM_DA) & (sm_col < SM_DA + N_HEADS)
    smt_row = lax.broadcasted_iota(jnp.int32, (SM_USED, L), 0)
    is_da_r = (smt_row >= SM_DA) & (smt_row < SM_DA + N_HEADS)
    blk_masks = [blk_ref[i] for i in range(nblk_masks)]

    def chunk(ci, carry):
        r0 = pl.multiple_of(ci * L, L)
        rows = pl.ds(r0, L)
        tc = sm_ref[rows, :]
        gc = jnp.where(is_da_c, -jnp.exp(arow_ref[...]) * _softplus(tc + drow_ref[...]), 0.0)
        bc = _sel_rows(tri_ref[...], gc, 3)
        tr = smt_ref[ci]
        gr = jnp.where(is_da_r, -jnp.exp(acol_ref[...]) * _softplus(tr + dcol_ref[...]), 0.0)
        br = _sel_cols(gr, trit_ref[...], 3)
        beta_all = pltpu.roll(_sigmoid(tc), SM_DA - SM_BETA, 1)
        eb_all = jnp.exp(bc)
        beb_all = beta_all * eb_all
        kdec_all = jnp.exp(bc[L - 1:L, :] - bc)
        def head(h):
            cols = slice(h * HEAD_DIM, (h + 1) * HEAD_DIM)
            lane = slice(SM_DA + h, SM_DA + h + 1)
            q = qc_sc[rows, cols]
            k = kc_sc[rows, cols]
            v = vc_sc[rows, cols]
            b_c = bc[:, lane]
            b_r = br[SM_DA + h:SM_DA + h + 1, :]
            beta = beta_all[:, lane]
            s_old = s_sc[h]

            kk = _dot_nt(k, k)
            qk_raw = _dot_nt(q, k)
            yield
            decay = jnp.where(causal, jnp.exp(jnp.minimum(b_c - b_r, 0.0)), 0.0)
            a = (beta * decay) * kk
            n = yield from _unit_lower_inverse_minus_eye(a, blk_masks)
            rhs = jnp.concatenate([beb_all[:, lane] * k, beta * v], axis=1)
            n_rhs = _dot(n, rhs)
            yield
            wu = rhs + n_rhs
            w, u = wu[:, :HEAD_DIM], wu[:, HEAD_DIM:]
            ws_qs = _dot(jnp.concatenate([w, q * eb_all[:, lane]], axis=0), s_old)
            yield
            v_new = u - ws_qs[:L]
            qk_v = _dot(qk_raw * decay, v_new)
            k_v = _dot_tn(k * kdec_all[:, lane], v_new)
            yield
            o = ws_qs[L:] + qk_v
            s_sc[h] = eb_all[L - 1:L, lane] * s_old + k_v
            hn = _head_rmsnorm(o, gh_ref[:, cols]) * _silu(z_ref[rows, cols])
            out_ref[rows, cols] = hn.astype(out_ref.dtype)

        _round_robin(head(h) for h in range(N_HEADS))
        return carry

    lax.fori_loop(0, nchunk, chunk, 0)


def _gdn(P, sm, smt, cw, arow, acol, drow, dcol, gh, tri, trit, blk, B, S, tb):
    T = B * S
    nblk = S // tb
    in_specs = _mixer_specs(tb, nblk, (PB_DQ, PB_DK, PB_DV, PB_DZ)) + [
        _full_spec((CONV_K, 3 * MIX_W)),
        _full_spec((1, LANES)), _full_spec((SM_USED, CHUNK)),
        _full_spec((1, LANES)), _full_spec((SM_USED, CHUNK)),
        _full_spec((1, MIX_W)),
        _full_spec((CHUNK, CHUNK)), _full_spec((CHUNK, CHUNK)), _full_spec(blk.shape)]
    return pl.pallas_call(
        _gdn_kernel,
        grid=(B, nblk),
        in_specs=in_specs,
        out_specs=pl.BlockSpec((tb, MIX_W), lambda b, c: (b * nblk + c, 0)),
        out_shape=jax.ShapeDtypeStruct((T, MIX_W), BF16),
        scratch_shapes=[pltpu.VMEM((N_HEADS, HEAD_DIM, HEAD_DIM), F32),
                        pltpu.VMEM((3, SUBLANES, MIX_W), F32),
                        pltpu.VMEM((tb, MIX_W), F32),
                        pltpu.VMEM((tb, MIX_W), F32),
                        pltpu.VMEM((tb, MIX_W), F32)],
        compiler_params=pltpu.CompilerParams(
            dimension_semantics=("arbitrary", "arbitrary"), vmem_limit_bytes=VMEM_LIMIT),
        name="gdn",
    )(P, P, P, P, sm, smt, cw, arow, acol, drow, dcol, gh, tri, trit, blk)


def _merge_kernel(x_ref, hm_ref, hl_ref, hd_ref, gates_ref, wup_ref, wout_ref, out_ref):
    d = x_ref.shape[1]
    acc = None
    for n, h_ref in enumerate((hm_ref, hl_ref, hd_ref)):
        up = jnp.dot(h_ref[...], wup_ref[n], preferred_element_type=F32)
        t = _sigmoid(gates_ref[:, n * d:(n + 1) * d]) * up
        acc = t if acc is None else acc + t
    out_ref[...] = x_ref[...] + jnp.dot(acc.astype(BF16), wout_ref[...], preferred_element_type=F32)


def _merge(x2, hm, hl, hd, P, wup, wout, tm):
    T, D = x2.shape
    gate_blk = (PB_GATES * MIX_W) // (N_BRANCH * D)
    row = lambda i: (i, 0)
    return pl.pallas_call(
        _merge_kernel,
        grid=(T // tm,),
        in_specs=[pl.BlockSpec((tm, D), row), pl.BlockSpec((tm, MIX_W), row),
                  pl.BlockSpec((tm, MIX_W), row), pl.BlockSpec((tm, MIX_W), row),
                  pl.BlockSpec((tm, N_BRANCH * D), lambda i: (i, gate_blk)),
                  _full_spec((N_BRANCH, MIX_W, D)), _full_spec((D, D))],
        out_specs=pl.BlockSpec((tm, D), row),
        out_shape=jax.ShapeDtypeStruct((T, D), F32),
        compiler_params=pltpu.CompilerParams(
            dimension_semantics=("arbitrary",), vmem_limit_bytes=VMEM_LIMIT),
        name="merge",
    )(x2, hm, hl, hd, P, wup, wout)


def _mlp_kernel(x_ref, g_ref, w1_ref, w2_ref, out_ref, u_sc, acc_sc):
    f = pl.program_id(1)

    @pl.when(f == 0)
    def _():
        x = x_ref[...]
        u = x * lax.rsqrt(jnp.mean(x * x, axis=-1, keepdims=True) + EPS) * g_ref[...]
        u_sc[...] = u.astype(BF16)
        acc_sc[...] = jnp.zeros_like(acc_sc)

    hmid = jnp.maximum(jnp.dot(u_sc[...], w1_ref[...], preferred_element_type=F32), 0.0)
    acc_sc[...] += jnp.dot((hmid * hmid).astype(BF16), w2_ref[...], preferred_element_type=F32)

    @pl.when(f == pl.num_programs(1) - 1)
    def _():
        out_ref[...] = x_ref[...] + acc_sc[...]


def _mlp(x2, g_row, w1, w2, tm, tf):
    T, D = x2.shape
    F = w1.shape[1]
    return pl.pallas_call(
        _mlp_kernel,
        grid=(T // tm, F // tf),
        in_specs=[pl.BlockSpec((tm, D), lambda i, f: (i, 0)),
                  pl.BlockSpec((1, D), lambda i, f: (0, 0)),
                  pl.BlockSpec((D, tf), lambda i, f: (0, f)),
                  pl.BlockSpec((tf, D), lambda i, f: (f, 0))],
        out_specs=pl.BlockSpec((tm, D), lambda i, f: (i, 0)),
        out_shape=jax.ShapeDtypeStruct((T, D), F32),
        scratch_shapes=[pltpu.VMEM((tm, D), BF16), pltpu.VMEM((tm, D), F32)],
        compiler_params=pltpu.CompilerParams(
            dimension_semantics=("arbitrary", "arbitrary"), vmem_limit_bytes=VMEM_LIMIT),
        name="mlp",
    )(x2, g_row, w1, w2)


def _final_norm_kernel(x_ref, g_ref, out_ref):
    x = x_ref[...]
    out_ref[...] = x * lax.rsqrt(jnp.mean(x * x, axis=-1, keepdims=True) + EPS) * g_ref[...]


def _final_norm(x2, g_row, tm):
    T, D = x2.shape
    return pl.pallas_call(
        _final_norm_kernel,
        grid=(T // tm,),
        in_specs=[pl.BlockSpec((tm, D), lambda i: (i, 0)), pl.BlockSpec((1, D), lambda i: (0, 0))],
        out_specs=pl.BlockSpec((tm, D), lambda i: (i, 0)),
        out_shape=jax.ShapeDtypeStruct((T, D), F32),
        compiler_params=pltpu.CompilerParams(dimension_semantics=("arbitrary",)),
        name="final_norm",
    )(x2, g_row)


def _tile(n, want):
    t = min(n, want)
    while n % t:
        t //= 2
    return t


def _pad_row(vals, offset, width=LANES):
    return jnp.zeros((1, width), F32).at[0, offset:offset + vals.shape[0]].set(vals.astype(F32))


def kernel(x, w_in, b_if, w_gla_lr, b_gla, conv_gdn, a_log, dt_bias, g_norm_mix, g_norm_mlp,
           g_head_mlstm, g_head_gla, g_head_gdn, w_up, w_out, w_mlp_in, w_mlp_out, g_final):
    B, S, D = x.shape
    depth = w_in.shape[0]
    T = B * S
    H, W = N_HEADS, MIX_W
    assert S % CHUNK == 0 and D == D_MODEL

    tri_np, lvm_np, lvmask_np = _np_consts(CHUNK)
    tri = jnp.asarray(tri_np, BF16)
    trit = jnp.asarray(tri_np.T, BF16)
    n_low = int(np.log2(SUBLANES))
    segm = jnp.asarray(np.concatenate(list(lvm_np[:n_low]) + [tri_np], axis=0), BF16)
    lvmask = jnp.asarray(lvmask_np, F32)
    ti = np.arange(CHUNK)[:, None]
    ui = np.arange(CHUNK)[None, :]
    blk_list = [((ti // SOLVE_BASE) == (ui // SOLVE_BASE)) & (ui < ti)]
    c = SOLVE_BASE
    while c < CHUNK:
        blk_list.append(((ti // (2 * c)) == (ui // (2 * c))) & ((ti // c) != (ui // c)))
        c *= 2
    blk = jnp.asarray(np.stack(blk_list), F32)

    sizes = [W, W, W, W, H, H, W, W, W, W, GLA_RANK, W, W, W, W, H, H, N_BRANCH * D]
    offs = np.concatenate([[0], np.cumsum(sizes)])
    big_ids = [0, 1, 2, 3, 6, 7, 8, 9, 11, 12, 13, 14, 17]
    small_ids = [4, 5, 10, 15, 16]

    tm_proj = _tile(T, 1024)
    tn_proj = 1536
    tb = _tile(S, 512)
    tm_merge = _tile(T, 512)
    tm_mlp = _tile(T, 1024)
    tf_mlp = _tile(w_mlp_in.shape[2], 1024)

    x2 = x.reshape(T, D)
    for l in range(depth):
        wl = w_in[l]
        w_big = jnp.concatenate([wl[:, offs[i]:offs[i + 1]] for i in big_ids], axis=1).astype(BF16)
        w_small = jnp.concatenate([wl[:, offs[i]:offs[i + 1]] for i in small_ids], axis=1)
        w_sm = jnp.pad(w_small, ((0, 0), (0, LANES - SM_USED))).astype(BF16)
        w_smt = w_small.T.astype(BF16)
        P, sm, smt = _proj(x2, g_norm_mix[l].reshape(1, D), w_big, w_sm, w_smt, tm_proj, tn_proj)

        bif_row = _pad_row(b_if[l], SM_LI)
        bif_col = jnp.broadcast_to(bif_row[0, :SM_USED, None], (SM_USED, CHUNK))
        hm = _mlstm(P, sm, smt, bif_row, bif_col, g_head_mlstm[l].reshape(1, W), tri, trit, B, S, tb)

        wlr = jnp.zeros((LANES, W), F32).at[SM_LLR:SM_LLR + GLA_RANK].set(w_gla_lr[l]).astype(BF16)
        hl = _gla(P, sm, wlr, b_gla[l].reshape(1, W), g_head_gla[l].reshape(1, W),
                  segm, lvmask, B, S, tb)

        arow = _pad_row(a_log[l], SM_DA)
        drow = _pad_row(dt_bias[l], SM_DA)
        acol = jnp.broadcast_to(arow[0, :SM_USED, None], (SM_USED, CHUNK))
        dcol = jnp.broadcast_to(drow[0, :SM_USED, None], (SM_USED, CHUNK))
        hd = _gdn(P, sm, smt, conv_gdn[l], arow, acol, drow, dcol, g_head_gdn[l].reshape(1, W),
                  tri, trit, blk, B, S, tb)

        x2 = _merge(x2, hm, hl, hd, P, w_up[l].astype(BF16), w_out[l].astype(BF16), tm_merge)
        x2 = _mlp(x2, g_norm_mlp[l].reshape(1, D), w_mlp_in[l].astype(BF16),
                  w_mlp_out[l].astype(BF16), tm_mlp, tf_mlp)
    out = _final_norm(x2, g_final.reshape(1, D), _tile(T, 1024))
    return out.reshape(B, S, D)
```

```python
import numpy as np
import jax
import jax.numpy as jnp
from jax import lax
from jax.experimental import pallas as pl
from jax.experimental.pallas import tpu as pltpu

F32 = jnp.float32
BF16 = jnp.bfloat16

N_HEADS = 4
HEAD_DIM = 128
MIX_W = N_HEADS * HEAD_DIM
N_BRANCH = 3
GLA_RANK = 16
GLA_TAU = 16.0
CONV_K = 4
EPS = 1e-6
D_MODEL = 1024

LANES = 128
SUBLANES = 8
BF16_ROWS = 16
CHUNK = 128
SOLVE_BASE = 16
NEG_BIG = -1e30
VMEM_LIMIT = 56 * 1024 * 1024

SM_LI, SM_LF, SM_LLR, SM_BETA, SM_DA = 0, 4, 8, 24, 28
SM_USED = 32
GROUP_W = 4 * MIX_W
N_BIG = 3 * GROUP_W + N_BRANCH * D_MODEL


def _dot(a, b):
    return jnp.dot(a.astype(BF16), b.astype(BF16), preferred_element_type=F32)


def _dot_nt(a, b):
    return lax.dot_general(a.astype(BF16), b.astype(BF16), (((1,), (1,)), ((), ())),
                           preferred_element_type=F32)


def _dot_tn(a, b):
    return lax.dot_general(a.astype(BF16), b.astype(BF16), (((0,), (0,)), ((), ())),
                           preferred_element_type=F32)


def _split_bf16(x, n):
    parts = []
    r = x
    for i in range(n):
        p = r.astype(BF16)
        parts.append(p)
        if i + 1 < n:
            r = r - p.astype(F32)
    return parts


def _sel_rows(m01, x, n):
    acc = None
    for p in _split_bf16(x, n):
        t = jnp.dot(m01, p, preferred_element_type=F32)
        acc = t if acc is None else acc + t
    return acc


def _sel_cols(x, m01, n):
    acc = None
    for p in _split_bf16(x, n):
        t = jnp.dot(p, m01, preferred_element_type=F32)
        acc = t if acc is None else acc + t
    return acc


def _log_sigmoid(x):
    return jnp.minimum(x, 0.0) - jnp.log(1.0 + jnp.exp(-jnp.abs(x)))


def _softplus(x):
    return jnp.maximum(x, 0.0) + jnp.log(1.0 + jnp.exp(-jnp.abs(x)))


def _sigmoid(x):
    return 1.0 / (1.0 + jnp.exp(-x))


def _silu(x):
    return x * _sigmoid(x)


def _head_rmsnorm(h, g_row):
    return h * lax.rsqrt(jnp.mean(h * h, axis=-1, keepdims=True) + EPS) * g_row


def _round_robin(gens):
    live = list(gens)
    while live:
        nxt = []
        for g in live:
            try:
                next(g)
                nxt.append(g)
            except StopIteration:
                pass
        live = nxt


def _np_consts(L):
    t = np.arange(L)[:, None]
    u = np.arange(L)[None, :]
    tri = (u <= t)
    lv_m, lv_mask = [], []
    c = 1
    while c < L:
        base = (t // (2 * c)) * (2 * c)
        ref = base + c - 1
        second = t >= base + c
        m = np.where(second, (u > ref) & (u <= t), (u > t) & (u <= ref))
        ub = (u // (2 * c)) * (2 * c)
        mask = (base == ub) & second & (u < ub + c)
        lv_m.append(m)
        lv_mask.append(mask)
        c *= 2
    return tri, np.stack(lv_m), np.stack(lv_mask)


def _full_spec(shape):
    nd = len(shape)
    return pl.BlockSpec(shape, lambda *_: (0,) * nd)


def _proj_kernel(x_ref, g_ref, wb_ref, ws_ref, wst_ref, p_ref, sm_ref, smt_ref, u_sc):
    j = pl.program_id(1)

    @pl.when(j == 0)
    def _():
        x = x_ref[...]
        u = x * lax.rsqrt(jnp.mean(x * x, axis=-1, keepdims=True) + EPS) * g_ref[...]
        ub = u.astype(BF16)
        u_sc[...] = ub
        sm_ref[...] = jnp.dot(ub, ws_ref[...], preferred_element_type=F32)
        smt = lax.dot_general(wst_ref[...], ub, (((1,), (1,)), ((), ())),
                              preferred_element_type=F32)
        for c in range(smt_ref.shape[0]):
            smt_ref[c] = smt[:, c * CHUNK:(c + 1) * CHUNK]

    p_ref[...] = jnp.dot(u_sc[...], wb_ref[...], preferred_element_type=F32).astype(p_ref.dtype)


def _proj(x2, g_row, w_big, w_sm, w_smt, tm, tn):
    T, D = x2.shape
    return pl.pallas_call(
        _proj_kernel,
        grid=(T // tm, N_BIG // tn),
        in_specs=[
            pl.BlockSpec((tm, D), lambda i, j: (i, 0)),
            pl.BlockSpec((1, D), lambda i, j: (0, 0)),
            pl.BlockSpec((D, tn), lambda i, j: (0, j)),
            pl.BlockSpec((D, LANES), lambda i, j: (0, 0)),
            pl.BlockSpec((SM_USED, D), lambda i, j: (0, 0)),
        ],
        out_specs=[
            pl.BlockSpec((tm, tn), lambda i, j: (i, j)),
            pl.BlockSpec((tm, LANES), lambda i, j: (i, 0)),
            pl.BlockSpec((tm // CHUNK, SM_USED, CHUNK), lambda i, j: (i, 0, 0)),
        ],
        out_shape=[
            jax.ShapeDtypeStruct((T, N_BIG), BF16),
            jax.ShapeDtypeStruct((T, LANES), F32),
            jax.ShapeDtypeStruct((T // CHUNK, SM_USED, CHUNK), F32),
        ],
        scratch_shapes=[pltpu.VMEM((tm, D), BF16)],
        compiler_params=pltpu.CompilerParams(
            dimension_semantics=("arbitrary", "arbitrary"), vmem_limit_bytes=VMEM_LIMIT),
        name="proj",
    )(x2, g_row, w_big, w_sm, w_smt)


def _unit_lower_inverse_minus_eye(a, blk_masks):
    a0 = a * blk_masks[0]
    n = -a0
    pw = _dot(a0, a0)
    yield
    span = 2
    while span < SOLVE_BASE:
        n_pw = _dot(n, pw)
        pw_next = _dot(pw, pw) if 2 * span < SOLVE_BASE else None
        yield
        n = n + pw + n_pw
        pw = pw_next
        span *= 2
    for m in blk_masks[1:]:
        c = a * m
        n_c = _dot(n, c)
        yield
        t = c + n_c
        t_n = _dot(t, n)
        yield
        n = n - t - t_n
    return n


def _mixers_kernel(pm_ref, pg_ref, pd_ref, sm_ref, smt_ref,
                   bifr_ref, bifc_ref, ghm_ref,
                   wlr_ref, bg_ref, ghl_ref, segm_ref, lvmask_ref,
                   cw_ref, arow_ref, acol_ref, drow_ref, dcol_ref, ghd_ref, blk_ref,
                   tri_ref, trit_ref,
                   out_ref,
                   c_sc, n_sc, m_sc, st_sc, s_sc, tail_sc):
    L = CHUNK
    W = MIX_W
    tb = pm_ref.shape[0]
    nchunk = tb // L
    nlev = lvmask_ref.shape[0]
    nlow = segm_ref.shape[0] // L - 1
    blk_masks = [blk_ref[i] for i in range(blk_ref.shape[0])]

    @pl.when(pl.program_id(1) == 0)
    def _():
        c_sc[...] = jnp.zeros_like(c_sc)
        n_sc[...] = jnp.zeros_like(n_sc)
        m_sc[...] = jnp.zeros_like(m_sc)
        st_sc[...] = jnp.zeros_like(st_sc)
        s_sc[...] = jnp.zeros_like(s_sc)
        tail_sc[...] = jnp.zeros_like(tail_sc)

    row_i = lax.broadcasted_iota(jnp.int32, (L, L), 0)
    col_i = lax.broadcasted_iota(jnp.int32, (L, L), 1)
    causal = col_i <= row_i
    sm_col = lax.broadcasted_iota(jnp.int32, (L, LANES), 1)
    is_lf_c = (sm_col >= SM_LF) & (sm_col < SM_LF + N_HEADS)
    is_da_c = (sm_col >= SM_DA) & (sm_col < SM_DA + N_HEADS)
    smt_row = lax.broadcasted_iota(jnp.int32, (SM_USED, L), 0)
    is_lf_r = (smt_row >= SM_LF) & (smt_row < SM_LF + N_HEADS)
    is_da_r = (smt_row >= SM_DA) & (smt_row < SM_DA + N_HEADS)

    def chunk(ci, carry):
        r0 = pl.multiple_of(ci * L, L)
        rows = pl.ds(r0, L)
        prev_rows = pl.ds(pl.multiple_of(jnp.maximum(r0 - BF16_ROWS, 0), BF16_ROWS), BF16_ROWS)

        sm = sm_ref[rows, :]
        tc = sm + bifr_ref[...]
        gdec_c = -jnp.exp(arow_ref[...]) * _softplus(sm + drow_ref[...])
        gc = jnp.where(is_lf_c, _log_sigmoid(tc), jnp.where(is_da_c, gdec_c, 0.0))
        bc = _sel_rows(tri_ref[...], gc, 3)
        smt = smt_ref[ci]
        tr = smt + bifc_ref[...]
        gdec_r = -jnp.exp(acol_ref[...]) * _softplus(smt + dcol_ref[...])
        gr = jnp.where(is_lf_r, _log_sigmoid(tr), jnp.where(is_da_r, gdec_r, 0.0))
        br = _sel_cols(gr, trit_ref[...], 3)
        beta_all = pltpu.roll(_sigmoid(sm), SM_DA - SM_BETA, 1)
        eb_all = jnp.exp(bc)
        beb_all = beta_all * eb_all
        kdec_all = jnp.exp(bc[L - 1:L, :] - bc)
        gpre = _dot(sm, wlr_ref[...]) + bg_ref[...]
        gneg_all = _log_sigmoid(gpre) * (-1.0 / GLA_TAU)

        def mlstm_head(h):
            cols = slice(h * HEAD_DIM, (h + 1) * HEAD_DIM)
            q = pm_ref[rows, h * HEAD_DIM:(h + 1) * HEAD_DIM]
            k = pm_ref[rows, W + h * HEAD_DIM:W + (h + 1) * HEAD_DIM]
            v = pm_ref[rows, 2 * W + h * HEAD_DIM:2 * W + (h + 1) * HEAD_DIM]
            b_c = bc[:, SM_LF + h:SM_LF + h + 1]
            li_c = tc[:, SM_LI + h:SM_LI + h + 1]
            b_r = br[SM_LF + h:SM_LF + h + 1, :]
            li_r = tr[SM_LI + h:SM_LI + h + 1, :]
            m_old = m_sc[h][0:1, 0:1]
            n_old = n_sc[h][0:1, :]
            c_old = c_sc[h]

            s_qk = _dot_nt(q, k)
            q_c = _dot(q, c_old)
            yield
            d = jnp.where(causal, b_c - b_r + li_r, NEG_BIG)
            inter = b_c + m_old
            m_t = jnp.maximum(inter, jnp.max(d, axis=1, keepdims=True))
            p = jnp.exp(d - m_t) * s_qk
            w_inter = jnp.exp(inter - m_t)
            b_last = b_c[L - 1:L, :]
            g_s = b_last - b_c + li_c
            m_new = jnp.maximum(b_last + m_old, jnp.max(g_s, axis=0, keepdims=True))
            w_s = jnp.exp(g_s - m_new)
            w_old = jnp.exp(b_last + m_old - m_new)
            kw = k.astype(F32) * w_s
            pv = _dot(p, v)
            kv = _dot_tn(kw, v)
            yield
            num = pv + w_inter * q_c
            den = (jnp.sum(p, axis=1, keepdims=True)
                   + w_inter * jnp.sum(q.astype(F32) * n_old, axis=1, keepdims=True))
            hout = num / jnp.maximum(jnp.abs(den), jnp.exp(-m_t))
            c_sc[h] = w_old * c_old + kv
            n_new = w_old * n_old + jnp.sum(kw, axis=0, keepdims=True)
            n_sc[h] = jnp.broadcast_to(n_new, (SUBLANES, LANES))
            m_sc[h] = jnp.broadcast_to(m_new, (SUBLANES, LANES))
            ogate = pm_ref[rows, 3 * W + h * HEAD_DIM:3 * W + (h + 1) * HEAD_DIM].astype(F32)
            gated = _sigmoid(ogate) * hout
            out_ref[rows, cols] = _head_rmsnorm(gated, ghm_ref[:, cols]).astype(out_ref.dtype)

        def gla_head(h):
            cols = slice(h * HEAD_DIM, (h + 1) * HEAD_DIM)
            q = pg_ref[rows, h * HEAD_DIM:(h + 1) * HEAD_DIM].astype(F32)
            k = pg_ref[rows, W + h * HEAD_DIM:W + (h + 1) * HEAD_DIM].astype(F32)
            v = pg_ref[rows, 2 * W + h * HEAD_DIM:2 * W + (h + 1) * HEAD_DIM]
            hi, lo = _split_bf16(gneg_all[:, cols], 2)
            gcat = jnp.concatenate([hi, lo], axis=1)
            e2 = jnp.dot(segm_ref[...], gcat, preferred_element_type=F32)
            yield
            e = e2[:, :HEAD_DIM] + e2[:, HEAD_DIM:]
            bsum = e[nlow * L:(nlow + 1) * L]
            bup = bsum[L - 1:L, :] - bsum
            st_old = st_sc[h]
            q_st = _dot_nt(q * jnp.exp(-bsum), st_old)
            vk = _dot_tn(v, k * jnp.exp(-bup))
            yield
            st_sc[h] = jnp.exp(-bsum[L - 1:L, :]) * st_old + vk
            a = jnp.zeros((L, L), F32)
            for lv in range(nlev):
                if lv < nlow:
                    dist = e[lv * L:(lv + 1) * L]
                else:
                    half = 1 << lv
                    b3 = bsum.reshape(L // (2 * half), 2 * half, HEAD_DIM)
                    dist = jnp.abs(b3 - b3[:, half - 1:half, :]).reshape(L, HEAD_DIM)
                ex = jnp.exp(-dist)
                a = a + lvmask_ref[lv] * _dot_nt(q * ex, k * ex)
                yield
            dqk = jnp.sum(q * k, axis=1, keepdims=True)
            o = _dot(a, v) + dqk * v.astype(F32) + q_st
            yield
            rgate = pg_ref[rows, 3 * W + h * HEAD_DIM:3 * W + (h + 1) * HEAD_DIM].astype(F32)
            hn = _head_rmsnorm(o, ghl_ref[:, cols]) * _silu(rgate)
            out_ref[rows, W + h * HEAD_DIM:W + (h + 1) * HEAD_DIM] = hn.astype(out_ref.dtype)

        def gdn_conv(piece, h, l2_mult):
            c0 = piece * W + h * HEAD_DIM
            x = pd_ref[rows, c0:c0 + HEAD_DIM].astype(F32)
            prev = jnp.where(ci == 0, tail_sc[:, c0:c0 + HEAD_DIM].astype(F32),
                             pd_ref[prev_rows, c0:c0 + HEAD_DIM].astype(F32))
            z = jnp.concatenate([prev[BF16_ROWS - SUBLANES:, :], x], axis=0)
            w = cw_ref[:, c0:c0 + HEAD_DIM]
            y = x * w[CONV_K - 1:CONV_K, :]
            for dlt in range(1, CONV_K):
                y = y + z[SUBLANES - dlt:SUBLANES - dlt + L, :] * w[CONV_K - 1 - dlt:CONV_K - dlt, :]
            y = _silu(y)
            if l2_mult is not None:
                y = y * (lax.rsqrt(jnp.sum(y * y, axis=-1, keepdims=True) + EPS) * l2_mult)
            return y

        def gdn_head(h):
            cols = slice(h * HEAD_DIM, (h + 1) * HEAD_DIM)
            lane = slice(SM_DA + h, SM_DA + h + 1)
            q = gdn_conv(0, h, HEAD_DIM ** -0.5)
            k = gdn_conv(1, h, 1.0)
            v = gdn_conv(2, h, None)
            b_c = bc[:, lane]
            b_r = br[SM_DA + h:SM_DA + h + 1, :]
            beta = beta_all[:, lane]
            s_old = s_sc[h]

            kk = _dot_nt(k, k)
            qk_raw = _dot_nt(q, k)
            yield
            decay = jnp.where(causal, jnp.exp(jnp.minimum(b_c - b_r, 0.0)), 0.0)
            a = (beta * decay) * kk
            n = yield from _unit_lower_inverse_minus_eye(a, blk_masks)
            rhs = jnp.concatenate([beb_all[:, lane] * k, beta * v], axis=1)
            n_rhs = _dot(n, rhs)
            yield
            wu = rhs + n_rhs
            w, u = wu[:, :HEAD_DIM], wu[:, HEAD_DIM:]
            ws_qs = _dot(jnp.concatenate([w, q * eb_all[:, lane]], axis=0), s_old)
            yield
            v_new = u - ws_qs[:L]
            qk_v = _dot(qk_raw * decay, v_new)
            k_v = _dot_tn(k * kdec_all[:, lane], v_new)
            yield
            o = ws_qs[L:] + qk_v
            s_sc[h] = eb_all[L - 1:L, lane] * s_old + k_v
            zgate = pd_ref[rows, 3 * W + h * HEAD_DIM:3 * W + (h + 1) * HEAD_DIM].astype(F32)
            hn = _head_rmsnorm(o, ghd_ref[:, cols]) * _silu(zgate)
            out_ref[rows, 2 * W + h * HEAD_DIM:2 * W + (h + 1) * HEAD_DIM] = hn.astype(out_ref.dtype)

        gens = []
        for h in range(N_HEADS):
            gens += [gdn_head(h), gla_head(h), mlstm_head(h)]
        _round_robin(gens)
        return carry

    lax.fori_loop(0, nchunk, chunk, 0)
    tail_sc[...] = pd_ref[tb - BF16_ROWS:tb, 0:3 * W]


def _mixers(P, sm, smt, consts, B, S, tb):
    T = B * S
    nblk = S // tb
    row_blk = lambda b, c: b * nblk + c
    in_specs = [pl.BlockSpec((tb, GROUP_W), lambda b, c, g=g: (row_blk(b, c), g)) for g in range(3)]
    in_specs.append(pl.BlockSpec((tb, LANES), lambda b, c: (row_blk(b, c), 0)))
    in_specs.append(pl.BlockSpec((tb // CHUNK, SM_USED, CHUNK), lambda b, c: (row_blk(b, c), 0, 0)))
    in_specs += [_full_spec(a.shape) for a in consts]
    state = pltpu.VMEM((N_HEADS, HEAD_DIM, HEAD_DIM), F32)
    small = pltpu.VMEM((N_HEADS, SUBLANES, LANES), F32)
    return pl.pallas_call(
        _mixers_kernel,
        grid=(B, nblk),
        in_specs=in_specs,
        out_specs=pl.BlockSpec((tb, N_BRANCH * MIX_W), lambda b, c: (row_blk(b, c), 0)),
        out_shape=jax.ShapeDtypeStruct((T, N_BRANCH * MIX_W), BF16),
        scratch_shapes=[state, small, small, state, state,
                        pltpu.VMEM((BF16_ROWS, 3 * MIX_W), BF16)],
        compiler_params=pltpu.CompilerParams(
            dimension_semantics=("arbitrary", "arbitrary"), vmem_limit_bytes=VMEM_LIMIT),
        name="mixers",
    )(P, P, P, sm, smt, *consts)


def _merge_kernel(x_ref, h_ref, gates_ref, wup_ref, wout_ref, out_ref):
    d = x_ref.shape[1]
    acc = None
    for n in range(N_BRANCH):
        up = jnp.dot(h_ref[:, n * MIX_W:(n + 1) * MIX_W], wup_ref[n], preferred_element_type=F32)
        t = _sigmoid(gates_ref[:, n * d:(n + 1) * d].astype(F32)) * up
        acc = t if acc is None else acc + t
    out_ref[...] = x_ref[...] + jnp.dot(acc.astype(BF16), wout_ref[...], preferred_element_type=F32)


def _merge(x2, h, P, wup, wout, tm):
    T, D = x2.shape
    gate_blk = (3 * GROUP_W) // (N_BRANCH * D)
    row = lambda i: (i, 0)
    return pl.pallas_call(
        _merge_kernel,
        grid=(T // tm,),
        in_specs=[pl.BlockSpec((tm, D), row), pl.BlockSpec((tm, N_BRANCH * MIX_W), row),
                  pl.BlockSpec((tm, N_BRANCH * D), lambda i: (i, gate_blk)),
                  _full_spec((N_BRANCH, MIX_W, D)), _full_spec((D, D))],
        out_specs=pl.BlockSpec((tm, D), row),
        out_shape=jax.ShapeDtypeStruct((T, D), F32),
        compiler_params=pltpu.CompilerParams(
            dimension_semantics=("arbitrary",), vmem_limit_bytes=VMEM_LIMIT),
        name="merge",
    )(x2, h, P, wup, wout)


def _mlp_kernel(x_ref, g_ref, w1_ref, w2_ref, out_ref, u_sc, acc_sc):
    f = pl.program_id(1)

    @pl.when(f == 0)
    def _():
        x = x_ref[...]
        u = x * lax.rsqrt(jnp.mean(x * x, axis=-1, keepdims=True) + EPS) * g_ref[...]
        u_sc[...] = u.astype(BF16)
        acc_sc[...] = jnp.zeros_like(acc_sc)

    hmid = jnp.maximum(jnp.dot(u_sc[...], w1_ref[...], preferred_element_type=F32), 0.0)
    acc_sc[...] += jnp.dot((hmid * hmid).astype(BF16), w2_ref[...], preferred_element_type=F32)

    @pl.when(f == pl.num_programs(1) - 1)
    def _():
        out_ref[...] = x_ref[...] + acc_sc[...]


def _mlp(x2, g_row, w1, w2, tm, tf):
    T, D = x2.shape
    F = w1.shape[1]
    return pl.pallas_call(
        _mlp_kernel,
        grid=(T // tm, F // tf),
        in_specs=[pl.BlockSpec((tm, D), lambda i, f: (i, 0)),
                  pl.BlockSpec((1, D), lambda i, f: (0, 0)),
                  pl.BlockSpec((D, tf), lambda i, f: (0, f)),
                  pl.BlockSpec((tf, D), lambda i, f: (f, 0))],
        out_specs=pl.BlockSpec((tm, D), lambda i, f: (i, 0)),
        out_shape=jax.ShapeDtypeStruct((T, D), F32),
        scratch_shapes=[pltpu.VMEM((tm, D), BF16), pltpu.VMEM((tm, D), F32)],
        compiler_params=pltpu.CompilerParams(
            dimension_semantics=("arbitrary", "arbitrary"), vmem_limit_bytes=VMEM_LIMIT),
        name="mlp",
    )(x2, g_row, w1, w2)


def _final_norm_kernel(x_ref, g_ref, out_ref):
    x = x_ref[...]
    out_ref[...] = x * lax.rsqrt(jnp.mean(x * x, axis=-1, keepdims=True) + EPS) * g_ref[...]


def _final_norm(x2, g_row, tm):
    T, D = x2.shape
    return pl.pallas_call(
        _final_norm_kernel,
        grid=(T // tm,),
        in_specs=[pl.BlockSpec((tm, D), lambda i: (i, 0)), pl.BlockSpec((1, D), lambda i: (0, 0))],
        out_specs=pl.BlockSpec((tm, D), lambda i: (i, 0)),
        out_shape=jax.ShapeDtypeStruct((T, D), F32),
        compiler_params=pltpu.CompilerParams(dimension_semantics=("arbitrary",)),
        name="final_norm",
    )(x2, g_row)


def _tile(n, want):
    t = min(n, want)
    while n % t:
        t //= 2
    return t


def _pad_rows(vals, offset, width=LANES):
    out = jnp.zeros((vals.shape[0], 1, width), F32)
    return out.at[:, 0, offset:offset + vals.shape[1]].set(vals.astype(F32))


def _as_cols(rows):
    return jnp.broadcast_to(rows[:, 0, :SM_USED, None], (rows.shape[0], SM_USED, CHUNK))


def kernel(x, w_in, b_if, w_gla_lr, b_gla, conv_gdn, a_log, dt_bias, g_norm_mix, g_norm_mlp,
           g_head_mlstm, g_head_gla, g_head_gdn, w_up, w_out, w_mlp_in, w_mlp_out, g_final):
    B, S, D = x.shape
    depth = w_in.shape[0]
    T = B * S
    H, W = N_HEADS, MIX_W
    assert S % CHUNK == 0 and D == D_MODEL

    tri_np, lvm_np, lvmask_np = _np_consts(CHUNK)
    tri = jnp.asarray(tri_np, BF16)
    trit = jnp.asarray(tri_np.T, BF16)
    n_low = int(np.log2(SUBLANES))
    segm = jnp.asarray(np.concatenate(list(lvm_np[:n_low]) + [tri_np], axis=0), BF16)
    lvmask = jnp.asarray(lvmask_np, F32)
    ti = np.arange(CHUNK)[:, None]
    ui = np.arange(CHUNK)[None, :]
    blk_list = [((ti // SOLVE_BASE) == (ui // SOLVE_BASE)) & (ui < ti)]
    c = SOLVE_BASE
    while c < CHUNK:
        blk_list.append(((ti // (2 * c)) == (ui // (2 * c))) & ((ti // c) != (ui // c)))
        c *= 2
    blk = jnp.asarray(np.stack(blk_list), F32)

    sizes = [W, W, W, W, H, H, W, W, W, W, GLA_RANK, W, W, W, W, H, H, N_BRANCH * D]
    offs = np.concatenate([[0], np.cumsum(sizes)])
    piece = lambda i: w_in[:, :, offs[i]:offs[i + 1]]
    q_scale = HEAD_DIM ** -0.5
    w_big = jnp.concatenate(
        [piece(0) * q_scale, piece(1), piece(2), piece(3),
         piece(6) * q_scale, piece(7), piece(8), piece(9),
         piece(11), piece(12), piece(13), piece(14), piece(17)], axis=2).astype(BF16)
    w_small = jnp.concatenate([piece(i) for i in (4, 5, 10, 15, 16)], axis=2)
    w_sm = jnp.pad(w_small, ((0, 0), (0, 0), (0, LANES - SM_USED))).astype(BF16)
    w_smt = jnp.swapaxes(w_small, 1, 2).astype(BF16)
    w_up_b = w_up.astype(BF16)
    w_out_b = w_out.astype(BF16)
    w_mlp_in_b = w_mlp_in.astype(BF16)
    w_mlp_out_b = w_mlp_out.astype(BF16)

    bif_rows = _pad_rows(b_if, SM_LI)
    a_rows = _pad_rows(a_log, SM_DA)
    dt_rows = _pad_rows(dt_bias, SM_DA)
    bif_cols, a_cols, dt_cols = _as_cols(bif_rows), _as_cols(a_rows), _as_cols(dt_rows)
    wlr = jnp.zeros((depth, LANES, W), F32).at[:, SM_LLR:SM_LLR + GLA_RANK].set(w_gla_lr).astype(BF16)

    tm_proj = _tile(T, 1024)
    tn_proj = 1536
    tb = _tile(S, 512)
    tm_merge = _tile(T, 512)
    tm_mlp = _tile(T, 1024)
    tf_mlp = _tile(w_mlp_in.shape[2], 1024)

    x2 = x.reshape(T, D)
    for l in range(depth):
        P, sm, smt = _proj(x2, g_norm_mix[l].reshape(1, D), w_big[l], w_sm[l], w_smt[l],
                           tm_proj, tn_proj)
        consts = (bif_rows[l], bif_cols[l], g_head_mlstm[l].reshape(1, W),
                  wlr[l], b_gla[l].reshape(1, W), g_head_gla[l].reshape(1, W), segm, lvmask,
                  conv_gdn[l], a_rows[l], a_cols[l], dt_rows[l], dt_cols[l],
                  g_head_gdn[l].reshape(1, W), blk, tri, trit)
        h = _mixers(P, sm, smt, consts, B, S, tb)
        x2 = _merge(x2, h, P, w_up_b[l], w_out_b[l], tm_merge)
        x2 = _mlp(x2, g_norm_mlp[l].reshape(1, D), w_mlp_in_b[l], w_mlp_out_b[l], tm_mlp, tf_mlp)
    out = _final_norm(x2, g_final.reshape(1, D), _tile(T, 1024))
    return out.reshape(B, S, D)
```

```python
import numpy as np
import jax
import jax.numpy as jnp
from jax import lax
from jax.experimental import pallas as pl
from jax.experimental.pallas import tpu as pltpu

F32 = jnp.float32
BF16 = jnp.bfloat16

N_HEADS = 4
HEAD_DIM = 128
MIX_W = N_HEADS * HEAD_DIM
N_BRANCH = 3
GLA_RANK = 16
GLA_TAU = 16.0
CONV_K = 4
EPS = 1e-6
D_MODEL = 1024

LANES = 128
SUBLANES = 8
BF16_ROWS = 16
CHUNK = 128
SOLVE_BASE = 16
NEG_BIG = -1e30
LOG2E = 1.4426950408889634
VMEM_LIMIT = 56 * 1024 * 1024

SM_LI, SM_LF, SM_LLR, SM_BETA, SM_DA = 0, 4, 8, 24, 28
SM_USED = 32
GROUP_W = 4 * MIX_W
N_BIG = 3 * GROUP_W + N_BRANCH * D_MODEL


def _dot(a, b):
    return jnp.dot(a.astype(BF16), b.astype(BF16), preferred_element_type=F32)


def _dot_nt(a, b):
    return lax.dot_general(a.astype(BF16), b.astype(BF16), (((1,), (1,)), ((), ())),
                           preferred_element_type=F32)


def _dot_tn(a, b):
    return lax.dot_general(a.astype(BF16), b.astype(BF16), (((0,), (0,)), ((), ())),
                           preferred_element_type=F32)


def _split_bf16(x, n):
    parts = []
    r = x
    for i in range(n):
        p = r.astype(BF16)
        parts.append(p)
        if i + 1 < n:
            r = r - p.astype(F32)
    return parts


def _sel_rows(m01_rep, x, n):
    return jnp.dot(m01_rep, jnp.concatenate(_split_bf16(x, n), axis=0), preferred_element_type=F32)


def _sel_cols(x, m01_rep, n):
    return jnp.dot(jnp.concatenate(_split_bf16(x, n), axis=1), m01_rep, preferred_element_type=F32)


def _log_sigmoid(x):
    return jnp.minimum(x, 0.0) - jnp.log(1.0 + jnp.exp(-jnp.abs(x)))


def _softplus(x):
    return jnp.maximum(x, 0.0) + jnp.log(1.0 + jnp.exp(-jnp.abs(x)))


def _sigmoid(x):
    return 0.5 * jnp.tanh(0.5 * x) + 0.5


def _silu(x):
    h = 0.5 * x
    return h + h * jnp.tanh(h)


def _head_rmsnorm(h, g_row):
    return h * lax.rsqrt(jnp.mean(h * h, axis=-1, keepdims=True) + EPS) * g_row


def _round_robin(gens):
    live = list(gens)
    while live:
        nxt = []
        for g in live:
            try:
                next(g)
                nxt.append(g)
            except StopIteration:
                pass
        live = nxt


def _np_consts(L):
    t = np.arange(L)[:, None]
    u = np.arange(L)[None, :]
    tri = (u <= t)
    lv_m, lv_mask = [], []
    c = 1
    while c < L:
        base = (t // (2 * c)) * (2 * c)
        ref = base + c - 1
        second = t >= base + c
        m = np.where(second, (u > ref) & (u <= t), (u > t) & (u <= ref))
        ub = (u // (2 * c)) * (2 * c)
        mask = (base == ub) & second & (u < ub + c)
        lv_m.append(m)
        lv_mask.append(mask)
        c *= 2
    return tri, np.stack(lv_m), np.stack(lv_mask)


def _full_spec(shape):
    nd = len(shape)
    return pl.BlockSpec(shape, lambda *_: (0,) * nd)


def _proj_kernel(x_ref, g_ref, wb_ref, ws_ref, wst_ref, p_ref, sm_ref, smt_ref, u_sc):
    j = pl.program_id(1)

    @pl.when(j == 0)
    def _():
        x = x_ref[...]
        u = x * lax.rsqrt(jnp.mean(x * x, axis=-1, keepdims=True) + EPS) * g_ref[...]
        ub = u.astype(BF16)
        u_sc[...] = ub
        sm_ref[...] = jnp.dot(ub, ws_ref[...], preferred_element_type=F32)
        smt = lax.dot_general(wst_ref[...], ub, (((1,), (1,)), ((), ())),
                              preferred_element_type=F32)
        for c in range(smt_ref.shape[0]):
            smt_ref[c] = smt[:, c * CHUNK:(c + 1) * CHUNK]

    p_ref[...] = jnp.dot(u_sc[...], wb_ref[...], preferred_element_type=F32).astype(p_ref.dtype)


def _proj(x2, g_row, w_big, w_sm, w_smt, layer, tm, tn):
    T, D = x2.shape
    return pl.pallas_call(
        _proj_kernel,
        grid=(T // tm, N_BIG // tn),
        in_specs=[
            pl.BlockSpec((tm, D), lambda i, j: (i, 0)),
            pl.BlockSpec((1, D), lambda i, j: (0, 0)),
            pl.BlockSpec((None, D, tn), lambda i, j: (layer, 0, j)),
            pl.BlockSpec((None, D, LANES), lambda i, j: (layer, 0, 0)),
            pl.BlockSpec((None, SM_USED, D), lambda i, j: (layer, 0, 0)),
        ],
        out_specs=[
            pl.BlockSpec((tm, tn), lambda i, j: (i, j)),
            pl.BlockSpec((tm, LANES), lambda i, j: (i, 0)),
            pl.BlockSpec((tm // CHUNK, SM_USED, CHUNK), lambda i, j: (i, 0, 0)),
        ],
        out_shape=[
            jax.ShapeDtypeStruct((T, N_BIG), BF16),
            jax.ShapeDtypeStruct((T, LANES), F32),
            jax.ShapeDtypeStruct((T // CHUNK, SM_USED, CHUNK), F32),
        ],
        scratch_shapes=[pltpu.VMEM((tm, D), BF16)],
        compiler_params=pltpu.CompilerParams(
            dimension_semantics=("arbitrary", "arbitrary"), vmem_limit_bytes=VMEM_LIMIT),
        name="proj",
    )(x2, g_row, w_big, w_sm, w_smt)


def _unit_lower_inverse_minus_eye(a, blk_masks):
    a0 = a * blk_masks[0]
    n = -a0
    pw = _dot(a0, a0)
    yield
    span = 2
    while span < SOLVE_BASE:
        n_pw = _dot(n, pw)
        pw_next = _dot(pw, pw) if 2 * span < SOLVE_BASE else None
        yield
        n = n + pw + n_pw
        pw = pw_next
        span *= 2
    for m in blk_masks[1:]:
        c = a * m
        n_c = _dot(n, c)
        yield
        t = c + n_c
        t_n = _dot(t, n)
        yield
        n = n - t - t_n
    return n


def _mixers_kernel(pm_ref, pg_ref, pd_ref, sm_ref, smt_ref,
                   bifr_ref, bifc_ref, ghm_ref,
                   wlr_ref, bg_ref, ghl_ref, segm_ref, lvmask_ref, lvsign_ref,
                   cw_ref, arow_ref, acol_ref, drow_ref, dcol_ref, ghd_ref, blk_ref,
                   tri_ref, trit_ref,
                   out_ref,
                   c_sc, n_sc, m_sc, st_sc, s_sc, tail_sc):
    L = CHUNK
    W = MIX_W
    tb = pm_ref.shape[0]
    nchunk = tb // L
    nlev = lvmask_ref.shape[0]
    nlow = segm_ref.shape[0] // L - 1
    blk_masks = [blk_ref[i] for i in range(blk_ref.shape[0])]

    @pl.when(pl.program_id(1) == 0)
    def _():
        c_sc[...] = jnp.zeros_like(c_sc)
        n_sc[...] = jnp.zeros_like(n_sc)
        m_sc[...] = jnp.zeros_like(m_sc)
        st_sc[...] = jnp.zeros_like(st_sc)
        s_sc[...] = jnp.zeros_like(s_sc)
        tail_sc[...] = jnp.zeros_like(tail_sc)

    row_i = lax.broadcasted_iota(jnp.int32, (L, L), 0)
    col_i = lax.broadcasted_iota(jnp.int32, (L, L), 1)
    causal = col_i <= row_i
    sm_col = lax.broadcasted_iota(jnp.int32, (L, LANES), 1)
    is_lf_c = (sm_col >= SM_LF) & (sm_col < SM_LF + N_HEADS)
    is_da_c = (sm_col >= SM_DA) & (sm_col < SM_DA + N_HEADS)
    smt_row = lax.broadcasted_iota(jnp.int32, (SM_USED, L), 0)
    is_lf_r = (smt_row >= SM_LF) & (smt_row < SM_LF + N_HEADS)
    is_da_r = (smt_row >= SM_DA) & (smt_row < SM_DA + N_HEADS)

    def chunk(ci, carry):
        r0 = pl.multiple_of(ci * L, L)
        rows = pl.ds(r0, L)
        prev_rows = pl.ds(pl.multiple_of(jnp.maximum(r0 - BF16_ROWS, 0), BF16_ROWS), BF16_ROWS)

        sm = sm_ref[rows, :]
        tc = sm + bifr_ref[...]
        gdec_c = -jnp.exp(arow_ref[...]) * _softplus(sm + drow_ref[...])
        gc = jnp.where(is_lf_c, _log_sigmoid(tc), jnp.where(is_da_c, gdec_c, 0.0)) * LOG2E
        bc = _sel_rows(tri_ref[...], gc, 3)
        tc = tc * LOG2E
        smt = smt_ref[ci]
        tr = smt + bifc_ref[...]
        gdec_r = -jnp.exp(acol_ref[...]) * _softplus(smt + dcol_ref[...])
        gr = jnp.where(is_lf_r, _log_sigmoid(tr), jnp.where(is_da_r, gdec_r, 0.0)) * LOG2E
        br = _sel_cols(gr, trit_ref[...], 3)
        tr = tr * LOG2E
        beta_all = pltpu.roll(_sigmoid(sm), SM_DA - SM_BETA, 1)
        eb_all = jnp.exp2(bc)
        beb_all = beta_all * eb_all
        kdec_all = jnp.exp2(bc[L - 1:L, :] - bc)
        gpre = _dot(sm, wlr_ref[...]) + bg_ref[...]
        glog_all = _log_sigmoid(gpre) * (LOG2E / GLA_TAU)

        def mlstm_head(h):
            cols = slice(h * HEAD_DIM, (h + 1) * HEAD_DIM)
            q = pm_ref[rows, h * HEAD_DIM:(h + 1) * HEAD_DIM]
            k = pm_ref[rows, W + h * HEAD_DIM:W + (h + 1) * HEAD_DIM]
            v = pm_ref[rows, 2 * W + h * HEAD_DIM:2 * W + (h + 1) * HEAD_DIM]
            b_c = bc[:, SM_LF + h:SM_LF + h + 1]
            li_c = tc[:, SM_LI + h:SM_LI + h + 1]
            b_r = br[SM_LF + h:SM_LF + h + 1, :]
            li_r = tr[SM_LI + h:SM_LI + h + 1, :]
            m_old = m_sc[h][0:1, 0:1]
            n_old = n_sc[h][0:1, :]
            c_old = c_sc[h]

            s_qk = _dot_nt(q, k)
            q_c = _dot(q, c_old)
            yield
            d = jnp.where(causal, b_c - b_r + li_r, NEG_BIG)
            inter = b_c + m_old
            m_t = jnp.maximum(inter, jnp.max(d, axis=1, keepdims=True))
            p = jnp.exp2(d - m_t) * s_qk
            w_inter = jnp.exp2(inter - m_t)
            b_last = b_c[L - 1:L, :]
            g_s = b_last - b_c + li_c
            m_new = jnp.maximum(b_last + m_old, jnp.max(g_s, axis=0, keepdims=True))
            w_s = jnp.exp2(g_s - m_new)
            w_old = jnp.exp2(b_last + m_old - m_new)
            kw = k.astype(F32) * w_s
            pv = _dot(p, v)
            kv = _dot_tn(kw, v)
            yield
            num = pv + w_inter * q_c
            den = (jnp.sum(p, axis=1, keepdims=True)
                   + w_inter * jnp.sum(q.astype(F32) * n_old, axis=1, keepdims=True))
            hout = num / jnp.maximum(jnp.abs(den), jnp.exp2(-m_t))
            c_sc[h] = w_old * c_old + kv
            n_new = w_old * n_old + jnp.sum(kw, axis=0, keepdims=True)
            n_sc[h] = jnp.broadcast_to(n_new, (SUBLANES, LANES))
            m_sc[h] = jnp.broadcast_to(m_new, (SUBLANES, LANES))
            ogate = pm_ref[rows, 3 * W + h * HEAD_DIM:3 * W + (h + 1) * HEAD_DIM].astype(F32)
            gated = _sigmoid(ogate) * hout
            out_ref[rows, cols] = _head_rmsnorm(gated, ghm_ref[:, cols]).astype(out_ref.dtype)

        def gla_head(h):
            cols = slice(h * HEAD_DIM, (h + 1) * HEAD_DIM)
            qb = pg_ref[rows, h * HEAD_DIM:(h + 1) * HEAD_DIM]
            kb = pg_ref[rows, W + h * HEAD_DIM:W + (h + 1) * HEAD_DIM]
            q = qb.astype(F32)
            k = kb.astype(F32)
            v = pg_ref[rows, 2 * W + h * HEAD_DIM:2 * W + (h + 1) * HEAD_DIM]
            e = _sel_rows(segm_ref[...], glog_all[:, cols], 2)
            yield
            bsum = e[nlow * L:(nlow + 1) * L]
            st_old = st_sc[h]
            q_st = _dot_nt(q * jnp.exp2(bsum), st_old)
            vk = _dot_tn(v, k * jnp.exp2(bsum[L - 1:L, :] - bsum))
            yield
            st_sc[h] = jnp.exp2(bsum[L - 1:L, :]) * st_old + vk
            a = jnp.zeros((L, L), F32)
            for lv in range(nlev):
                if lv < nlow:
                    lw = e[lv * L:(lv + 1) * L]
                else:
                    half = 1 << lv
                    b3 = bsum.reshape(L // (2 * half), 2 * half, HEAD_DIM)
                    lw = (b3 - b3[:, half - 1:half, :]).reshape(L, HEAD_DIM) * lvsign_ref[lv - nlow]
                ex = jnp.exp2(lw).astype(BF16)
                a = a + lvmask_ref[lv] * _dot_nt(qb * ex, kb * ex)
                yield
            dqk = jnp.sum(q * k, axis=1, keepdims=True)
            o = _dot(a, v) + dqk * v.astype(F32) + q_st
            yield
            rgate = pg_ref[rows, 3 * W + h * HEAD_DIM:3 * W + (h + 1) * HEAD_DIM].astype(F32)
            hn = _head_rmsnorm(o, ghl_ref[:, cols]) * _silu(rgate)
            out_ref[rows, W + h * HEAD_DIM:W + (h + 1) * HEAD_DIM] = hn.astype(out_ref.dtype)

        def gdn_conv(piece, h, l2_mult):
            c0 = piece * W + h * HEAD_DIM
            xb = pd_ref[rows, c0:c0 + HEAD_DIM]
            x = xb.astype(F32)
            w = cw_ref[:, c0:c0 + HEAD_DIM]
            prev = jnp.where(ci == 0, tail_sc[:, c0:c0 + HEAD_DIM].astype(F32),
                             pd_ref[prev_rows, c0:c0 + HEAD_DIM].astype(F32))
            z = jnp.concatenate([prev[BF16_ROWS - SUBLANES:, :], x], axis=0)
            y = x * w[CONV_K - 1:CONV_K, :]
            for dlt in range(1, CONV_K):
                y = y + z[SUBLANES - dlt:SUBLANES - dlt + L, :] * w[CONV_K - 1 - dlt:CONV_K - dlt, :]
            y = _silu(y)
            if l2_mult is not None:
                y = y * (lax.rsqrt(jnp.sum(y * y, axis=-1, keepdims=True) + EPS) * l2_mult)
            return y

        def gdn_head(h):
            cols = slice(h * HEAD_DIM, (h + 1) * HEAD_DIM)
            lane = slice(SM_DA + h, SM_DA + h + 1)
            q = gdn_conv(0, h, HEAD_DIM ** -0.5)
            k = gdn_conv(1, h, 1.0)
            v = gdn_conv(2, h, None)
            b_c = bc[:, lane]
            b_r = br[SM_DA + h:SM_DA + h + 1, :]
            beta = beta_all[:, lane]
            s_old = s_sc[h]

            kk = _dot_nt(k, k)
            yield
            decay = jnp.where(causal, jnp.exp2(jnp.minimum(b_c - b_r, 0.0)), 0.0)
            a = (beta * decay) * kk
            n = yield from _unit_lower_inverse_minus_eye(a, blk_masks)
            rhs = jnp.concatenate([beb_all[:, lane] * k, beta * v], axis=1)
            n_rhs = _dot(n, rhs)
            qk_raw = _dot_nt(q, k)
            yield
            wu = rhs + n_rhs
            w, u = wu[:, :HEAD_DIM], wu[:, HEAD_DIM:]
            ws_qs = _dot(jnp.concatenate([w, q * eb_all[:, lane]], axis=0), s_old)
            yield
            v_new = u - ws_qs[:L]
            qk_v = _dot(qk_raw * decay, v_new)
            k_v = _dot_tn(k * kdec_all[:, lane], v_new)
            yield
            o = ws_qs[L:] + qk_v
            s_sc[h] = eb_all[L - 1:L, lane] * s_old + k_v
            zgate = pd_ref[rows, 3 * W + h * HEAD_DIM:3 * W + (h + 1) * HEAD_DIM].astype(F32)
            hn = _head_rmsnorm(o, ghd_ref[:, cols]) * _silu(zgate)
            out_ref[rows, 2 * W + h * HEAD_DIM:2 * W + (h + 1) * HEAD_DIM] = hn.astype(out_ref.dtype)

        gens = [f(h) for f in (gdn_head, gla_head, mlstm_head) for h in range(N_HEADS)]
        _round_robin(gens)
        return carry

    lax.fori_loop(0, nchunk, chunk, 0)
    tail_sc[...] = pd_ref[tb - BF16_ROWS:tb, 0:3 * W]


def _mixers(P, sm, smt, consts, B, S, tb):
    T = B * S
    nblk = S // tb
    row_blk = lambda b, c: b * nblk + c
    in_specs = [pl.BlockSpec((tb, GROUP_W), lambda b, c, g=g: (row_blk(b, c), g)) for g in range(3)]
    in_specs.append(pl.BlockSpec((tb, LANES), lambda b, c: (row_blk(b, c), 0)))
    in_specs.append(pl.BlockSpec((tb // CHUNK, SM_USED, CHUNK), lambda b, c: (row_blk(b, c), 0, 0)))
    in_specs += [_full_spec(a.shape) for a in consts]
    state = pltpu.VMEM((N_HEADS, HEAD_DIM, HEAD_DIM), F32)
    small = pltpu.VMEM((N_HEADS, SUBLANES, LANES), F32)
    return pl.pallas_call(
        _mixers_kernel,
        grid=(B, nblk),
        in_specs=in_specs,
        out_specs=pl.BlockSpec((tb, N_BRANCH * MIX_W), lambda b, c: (row_blk(b, c), 0)),
        out_shape=jax.ShapeDtypeStruct((T, N_BRANCH * MIX_W), BF16),
        scratch_shapes=[state, small, small, state, state,
                        pltpu.VMEM((BF16_ROWS, 3 * MIX_W), BF16)],
        compiler_params=pltpu.CompilerParams(
            dimension_semantics=("arbitrary", "arbitrary"), vmem_limit_bytes=VMEM_LIMIT),
        name="mixers",
    )(P, P, P, sm, smt, *consts)


def _merge_kernel(x_ref, h_ref, gates_ref, wup_ref, wout_ref, out_ref):
    d = x_ref.shape[1]
    acc = None
    for n in range(N_BRANCH):
        up = jnp.dot(h_ref[:, n * MIX_W:(n + 1) * MIX_W], wup_ref[n], preferred_element_type=F32)
        t = _sigmoid(gates_ref[:, n * d:(n + 1) * d].astype(F32)) * up
        acc = t if acc is None else acc + t
    out_ref[...] = x_ref[...] + jnp.dot(acc.astype(BF16), wout_ref[...], preferred_element_type=F32)


def _merge(x2, h, P, wup, wout, layer, tm):
    T, D = x2.shape
    gate_blk = (3 * GROUP_W) // (N_BRANCH * D)
    row = lambda i: (i, 0)
    return pl.pallas_call(
        _merge_kernel,
        grid=(T // tm,),
        in_specs=[pl.BlockSpec((tm, D), row), pl.BlockSpec((tm, N_BRANCH * MIX_W), row),
                  pl.BlockSpec((tm, N_BRANCH * D), lambda i: (i, gate_blk)),
                  pl.BlockSpec((None, N_BRANCH, MIX_W, D), lambda i: (layer, 0, 0, 0)),
                  pl.BlockSpec((None, D, D), lambda i: (layer, 0, 0))],
        out_specs=pl.BlockSpec((tm, D), row),
        out_shape=jax.ShapeDtypeStruct((T, D), F32),
        compiler_params=pltpu.CompilerParams(
            dimension_semantics=("arbitrary",), vmem_limit_bytes=VMEM_LIMIT),
        name="merge",
    )(x2, h, P, wup, wout)


def _mlp_kernel(x_ref, g_ref, w1_ref, w2_ref, out_ref, u_sc, acc_sc):
    f = pl.program_id(1)

    @pl.when(f == 0)
    def _():
        x = x_ref[...]
        u = x * lax.rsqrt(jnp.mean(x * x, axis=-1, keepdims=True) + EPS) * g_ref[...]
        u_sc[...] = u.astype(BF16)
        acc_sc[...] = jnp.zeros_like(acc_sc)

    hmid = jnp.maximum(jnp.dot(u_sc[...], w1_ref[...], preferred_element_type=F32), 0.0)
    acc_sc[...] += jnp.dot((hmid * hmid).astype(BF16), w2_ref[...], preferred_element_type=F32)

    @pl.when(f == pl.num_programs(1) - 1)
    def _():
        out_ref[...] = x_ref[...] + acc_sc[...]


def _mlp(x2, g_row, w1, w2, layer, tm, tf):
    T, D = x2.shape
    F = w1.shape[2]
    return pl.pallas_call(
        _mlp_kernel,
        grid=(T // tm, F // tf),
        in_specs=[pl.BlockSpec((tm, D), lambda i, f: (i, 0)),
                  pl.BlockSpec((1, D), lambda i, f: (0, 0)),
                  pl.BlockSpec((None, D, tf), lambda i, f: (layer, 0, f)),
                  pl.BlockSpec((None, tf, D), lambda i, f: (layer, f, 0))],
        out_specs=pl.BlockSpec((tm, D), lambda i, f: (i, 0)),
        out_shape=jax.ShapeDtypeStruct((T, D), F32),
        scratch_shapes=[pltpu.VMEM((tm, D), BF16), pltpu.VMEM((tm, D), F32)],
        compiler_params=pltpu.CompilerParams(
            dimension_semantics=("arbitrary", "arbitrary"), vmem_limit_bytes=VMEM_LIMIT),
        name="mlp",
    )(x2, g_row, w1, w2)


def _final_norm_kernel(x_ref, g_ref, out_ref):
    x = x_ref[...]
    out_ref[...] = x * lax.rsqrt(jnp.mean(x * x, axis=-1, keepdims=True) + EPS) * g_ref[...]


def _final_norm(x2, g_row, tm):
    T, D = x2.shape
    return pl.pallas_call(
        _final_norm_kernel,
        grid=(T // tm,),
        in_specs=[pl.BlockSpec((tm, D), lambda i: (i, 0)), pl.BlockSpec((1, D), lambda i: (0, 0))],
        out_specs=pl.BlockSpec((tm, D), lambda i: (i, 0)),
        out_shape=jax.ShapeDtypeStruct((T, D), F32),
        compiler_params=pltpu.CompilerParams(dimension_semantics=("arbitrary",)),
        name="final_norm",
    )(x2, g_row)


def _tile(n, want):
    t = min(n, want)
    while n % t:
        t //= 2
    return t


def _pad_rows(vals, offset, width=LANES):
    out = jnp.zeros((vals.shape[0], 1, width), F32)
    return out.at[:, 0, offset:offset + vals.shape[1]].set(vals.astype(F32))


def _as_cols(rows):
    return jnp.broadcast_to(rows[:, 0, :SM_USED, None], (rows.shape[0], SM_USED, CHUNK))


def kernel(x, w_in, b_if, w_gla_lr, b_gla, conv_gdn, a_log, dt_bias, g_norm_mix, g_norm_mlp,
           g_head_mlstm, g_head_gla, g_head_gdn, w_up, w_out, w_mlp_in, w_mlp_out, g_final):
    B, S, D = x.shape
    depth = w_in.shape[0]
    T = B * S
    H, W = N_HEADS, MIX_W
    assert S % CHUNK == 0 and D == D_MODEL

    tri_np, lvm_np, lvmask_np = _np_consts(CHUNK)
    tri = jnp.asarray(np.tile(tri_np, (1, 3)), BF16)
    trit = jnp.asarray(np.tile(tri_np.T, (3, 1)), BF16)
    n_low = int(np.log2(SUBLANES))
    segm = jnp.asarray(np.tile(np.concatenate(list(lvm_np[:n_low]) + [tri_np], axis=0), (1, 2)), BF16)
    lvmask = jnp.asarray(lvmask_np, F32)
    ti = np.arange(CHUNK)[:, None]
    ui = np.arange(CHUNK)[None, :]
    lvsign = jnp.asarray(np.stack([
        np.broadcast_to(np.where((ti // (1 << lv)) % 2 == 1, 1.0, -1.0), (CHUNK, HEAD_DIM))
        for lv in range(n_low, lvmask_np.shape[0])]), F32)
    blk_list = [((ti // SOLVE_BASE) == (ui // SOLVE_BASE)) & (ui < ti)]
    c = SOLVE_BASE
    while c < CHUNK:
        blk_list.append(((ti // (2 * c)) == (ui // (2 * c))) & ((ti // c) != (ui // c)))
        c *= 2
    blk = jnp.asarray(np.stack(blk_list), F32)

    sizes = [W, W, W, W, H, H, W, W, W, W, GLA_RANK, W, W, W, W, H, H, N_BRANCH * D]
    offs = np.concatenate([[0], np.cumsum(sizes)])
    piece = lambda i: w_in[:, :, offs[i]:offs[i + 1]]
    q_scale = HEAD_DIM ** -0.5
    big = [piece(0) * q_scale, piece(1), piece(2), piece(3),
           piece(6) * q_scale, piece(7), piece(8), piece(9),
           piece(11), piece(12), piece(13), piece(14), piece(17)]
    w_big = jnp.concatenate([p.astype(BF16) for p in big], axis=2)
    w_small = jnp.concatenate([piece(i) for i in (4, 5, 10, 15, 16)], axis=2)
    w_sm = jnp.pad(w_small, ((0, 0), (0, 0), (0, LANES - SM_USED))).astype(BF16)
    w_smt = jnp.swapaxes(w_small, 1, 2).astype(BF16)
    w_up_b = w_up.astype(BF16)
    w_out_b = w_out.astype(BF16)
    w_mlp_in_b = w_mlp_in.astype(BF16)
    w_mlp_out_b = w_mlp_out.astype(BF16)

    bif_rows = _pad_rows(b_if, SM_LI)
    a_rows = _pad_rows(a_log, SM_DA)
    dt_rows = _pad_rows(dt_bias, SM_DA)
    bif_cols, a_cols, dt_cols = _as_cols(bif_rows), _as_cols(a_rows), _as_cols(dt_rows)
    wlr = jnp.zeros((depth, LANES, W), F32).at[:, SM_LLR:SM_LLR + GLA_RANK].set(w_gla_lr).astype(BF16)

    tm_proj = _tile(T, 1024)
    tn_proj = 1536
    tb = _tile(S, 512)
    tm_merge = _tile(T, 512)
    tm_mlp = _tile(T, 1024)
    tf_mlp = _tile(w_mlp_in.shape[2], 1024)

    x2 = x.reshape(T, D)
    for l in range(depth):
        P, sm, smt = _proj(x2, g_norm_mix[l].reshape(1, D), w_big, w_sm, w_smt, l,
                           tm_proj, tn_proj)
        consts = (bif_rows[l], bif_cols[l], g_head_mlstm[l].reshape(1, W),
                  wlr[l], b_gla[l].reshape(1, W), g_head_gla[l].reshape(1, W), segm, lvmask, lvsign,
                  conv_gdn[l], a_rows[l], a_cols[l], dt_rows[l], dt_cols[l],
                  g_head_gdn[l].reshape(1, W), blk, tri, trit)
        h = _mixers(P, sm, smt, consts, B, S, tb)
        x2 = _merge(x2, h, P, w_up_b, w_out_b, l, tm_merge)
        x2 = _mlp(x2, g_norm_mlp[l].reshape(1, D), w_mlp_in_b, w_mlp_out_b, l, tm_mlp, tf_mlp)
    out = _final_norm(x2, g_final.reshape(1, D), _tile(T, 1024))
    return out.reshape(B, S, D)
```

```python
import functools

import numpy as np
import jax
import jax.numpy as jnp
from jax import lax
from jax.experimental import pallas as pl
from jax.experimental.pallas import tpu as pltpu

F32 = jnp.float32
BF16 = jnp.bfloat16

N_HEADS = 4
HEAD_DIM = 128
MIX_W = N_HEADS * HEAD_DIM
N_BRANCH = 3
GLA_RANK = 16
GLA_TAU = 16.0
CONV_K = 4
EPS = 1e-6
D_MODEL = 1024

LANES = 128
SUBLANES = 8
BF16_ROWS = 16
CHUNK = 128
SOLVE_BASE = 16
NEG_BIG = -1e30
LOG2E = 1.4426950408889634
VMEM_LIMIT = 56 * 1024 * 1024

SM_LI, SM_LF, SM_LLR, SM_BETA, SM_DA = 0, 4, 8, 24, 28
SM_USED = 32
GROUP_W = 4 * MIX_W
N_BIG = 3 * GROUP_W + N_BRANCH * D_MODEL


def _dot(a, b):
    return jnp.dot(a.astype(BF16), b.astype(BF16), preferred_element_type=F32)


def _dot_nt(a, b):
    return lax.dot_general(a.astype(BF16), b.astype(BF16), (((1,), (1,)), ((), ())),
                           preferred_element_type=F32)


def _dot_tn(a, b):
    return lax.dot_general(a.astype(BF16), b.astype(BF16), (((0,), (0,)), ((), ())),
                           preferred_element_type=F32)


def _split_bf16(x, n):
    parts = []
    r = x
    for i in range(n):
        p = r.astype(BF16)
        parts.append(p)
        if i + 1 < n:
            r = r - p.astype(F32)
    return parts


def _sel_rows(m01_rep, x, n):
    return jnp.dot(m01_rep, jnp.concatenate(_split_bf16(x, n), axis=0), preferred_element_type=F32)


def _sel_cols(x, m01_rep, n):
    return jnp.dot(jnp.concatenate(_split_bf16(x, n), axis=1), m01_rep, preferred_element_type=F32)


def _log_sigmoid(x):
    return jnp.minimum(x, 0.0) - jnp.log(1.0 + jnp.exp(-jnp.abs(x)))


def _softplus(x):
    return jnp.maximum(x, 0.0) + jnp.log(1.0 + jnp.exp(-jnp.abs(x)))


def _sigmoid_of_twice(h):
    return 0.5 * jnp.tanh(h) + 0.5


def _silu_of_twice(h):
    return h + h * jnp.tanh(h)


def _sigmoid(x):
    return _sigmoid_of_twice(0.5 * x)


def _head_rmsnorm(h, g_row):
    return h * lax.rsqrt(jnp.mean(h * h, axis=-1, keepdims=True) + EPS) * g_row


def _round_robin(gens):
    live = list(gens)
    while live:
        nxt = []
        for g in live:
            try:
                next(g)
                nxt.append(g)
            except StopIteration:
                pass
        live = nxt


def _np_consts(L):
    t = np.arange(L)[:, None]
    u = np.arange(L)[None, :]
    tri = (u <= t)
    lv_m, lv_mask = [], []
    c = 1
    while c < L:
        base = (t // (2 * c)) * (2 * c)
        ref = base + c - 1
        second = t >= base + c
        m = np.where(second, (u > ref) & (u <= t), (u > t) & (u <= ref))
        ub = (u // (2 * c)) * (2 * c)
        mask = (base == ub) & second & (u < ub + c)
        lv_m.append(m)
        lv_mask.append(mask)
        c *= 2
    return tri, np.stack(lv_m), np.stack(lv_mask)


def _full_spec(shape):
    nd = len(shape)
    return pl.BlockSpec(shape, lambda *_: (0,) * nd)


def _regroup_kernel(w_ref, o_ref, *, pieces):
    for src, dst, n, scale in pieces:
        v = w_ref[:, src:src + n]
        if scale != 1.0:
            v = v * scale
        o_ref[:, dst:dst + n] = v.astype(o_ref.dtype)


def _regroup(w_in, pieces, n_out, tr):
    depth, D, n_in = w_in.shape
    return pl.pallas_call(
        functools.partial(_regroup_kernel, pieces=pieces),
        grid=(depth, D // tr),
        in_specs=[pl.BlockSpec((None, tr, n_in), lambda l, r: (l, r, 0))],
        out_specs=pl.BlockSpec((None, tr, n_out), lambda l, r: (l, r, 0)),
        out_shape=jax.ShapeDtypeStruct((depth, D, n_out), BF16),
        compiler_params=pltpu.CompilerParams(
            dimension_semantics=("arbitrary", "arbitrary"), vmem_limit_bytes=VMEM_LIMIT),
        name="regroup",
    )(w_in)


def _proj_kernel(x_ref, g_ref, wb_ref, ws_ref, wst_ref, p_ref, sm_ref, smt_ref, u_sc):
    j = pl.program_id(1)

    @pl.when(j == 0)
    def _():
        x = x_ref[...]
        u = x * lax.rsqrt(jnp.mean(x * x, axis=-1, keepdims=True) + EPS) * g_ref[...]
        ub = u.astype(BF16)
        u_sc[...] = ub
        sm_ref[...] = jnp.dot(ub, ws_ref[...], preferred_element_type=F32)
        smt = lax.dot_general(wst_ref[...], ub, (((1,), (1,)), ((), ())),
                              preferred_element_type=F32)
        for c in range(smt_ref.shape[0]):
            smt_ref[c] = smt[:, c * CHUNK:(c + 1) * CHUNK]

    p_ref[...] = jnp.dot(u_sc[...], wb_ref[...], preferred_element_type=F32).astype(p_ref.dtype)


def _proj(x2, g_row, w_big, w_sm, w_smt, layer, tm, tn):
    T, D = x2.shape
    return pl.pallas_call(
        _proj_kernel,
        grid=(T // tm, N_BIG // tn),
        in_specs=[
            pl.BlockSpec((tm, D), lambda i, j: (i, 0)),
            pl.BlockSpec((1, D), lambda i, j: (0, 0)),
            pl.BlockSpec((None, D, tn), lambda i, j: (layer, 0, j)),
            pl.BlockSpec((None, D, LANES), lambda i, j: (layer, 0, 0)),
            pl.BlockSpec((None, SM_USED, D), lambda i, j: (layer, 0, 0)),
        ],
        out_specs=[
            pl.BlockSpec((tm, tn), lambda i, j: (i, j)),
            pl.BlockSpec((tm, LANES), lambda i, j: (i, 0)),
            pl.BlockSpec((tm // CHUNK, SM_USED, CHUNK), lambda i, j: (i, 0, 0)),
        ],
        out_shape=[
            jax.ShapeDtypeStruct((T, N_BIG), BF16),
            jax.ShapeDtypeStruct((T, LANES), F32),
            jax.ShapeDtypeStruct((T // CHUNK, SM_USED, CHUNK), F32),
        ],
        scratch_shapes=[pltpu.VMEM((tm, D), BF16)],
        compiler_params=pltpu.CompilerParams(
            dimension_semantics=("arbitrary", "arbitrary"), vmem_limit_bytes=VMEM_LIMIT),
        name="proj",
    )(x2, g_row, w_big, w_sm, w_smt)


def _unit_lower_inverse_minus_eye(a, blk_masks):
    a0 = a * blk_masks[0]
    n = -a0
    pw = _dot(a0, a0)
    yield
    span = 2
    while span < SOLVE_BASE:
        n_pw = _dot(n, pw)
        pw_next = _dot(pw, pw) if 2 * span < SOLVE_BASE else None
        yield
        n = n + pw + n_pw
        pw = pw_next
        span *= 2
    for m in blk_masks[1:]:
        c = a * m
        n_c = _dot(n, c)
        yield
        t = c + n_c
        t_n = _dot(t, n)
        yield
        n = n - t - t_n
    return n


def _mixers_kernel(pm_ref, pg_ref, pd_ref, sm_ref, smt_ref,
                   bifr_ref, bifc_ref, ghm_ref,
                   wlr_ref, bg_ref, ghl_ref, segm_ref, lvmask_ref, lvsign_ref,
                   cw_ref, arow_ref, acol_ref, drow_ref, dcol_ref, ghd_ref, blk_ref,
                   tri_ref, trit_ref,
                   out_ref,
                   c_sc, n_sc, m_sc, st_sc, s_sc, tail_sc):
    L = CHUNK
    W = MIX_W
    tb = pm_ref.shape[0]
    nchunk = tb // L
    nlev = lvmask_ref.shape[0]
    nlow = segm_ref.shape[0] // L - 1
    blk_masks = [blk_ref[i] for i in range(blk_ref.shape[0])]

    @pl.when(pl.program_id(1) == 0)
    def _():
        c_sc[...] = jnp.zeros_like(c_sc)
        n_sc[...] = jnp.zeros_like(n_sc)
        m_sc[...] = jnp.zeros_like(m_sc)
        st_sc[...] = jnp.zeros_like(st_sc)
        s_sc[...] = jnp.zeros_like(s_sc)
        tail_sc[...] = jnp.zeros_like(tail_sc)

    row_i = lax.broadcasted_iota(jnp.int32, (L, L), 0)
    col_i = lax.broadcasted_iota(jnp.int32, (L, L), 1)
    causal = col_i <= row_i
    sm_col = lax.broadcasted_iota(jnp.int32, (L, LANES), 1)
    is_lf_c = (sm_col >= SM_LF) & (sm_col < SM_LF + N_HEADS)
    is_da_c = (sm_col >= SM_DA) & (sm_col < SM_DA + N_HEADS)
    smt_row = lax.broadcasted_iota(jnp.int32, (SM_USED, L), 0)
    is_lf_r = (smt_row >= SM_LF) & (smt_row < SM_LF + N_HEADS)
    is_da_r = (smt_row >= SM_DA) & (smt_row < SM_DA + N_HEADS)

    def chunk(ci, carry):
        r0 = pl.multiple_of(ci * L, L)
        rows = pl.ds(r0, L)
        prev_rows = pl.ds(pl.multiple_of(jnp.maximum(r0 - BF16_ROWS, 0), BF16_ROWS), BF16_ROWS)

        sm = sm_ref[rows, :]
        tc = sm + bifr_ref[...]
        gdec_c = -jnp.exp(arow_ref[...]) * _softplus(sm + drow_ref[...])
        gc = jnp.where(is_lf_c, _log_sigmoid(tc), jnp.where(is_da_c, gdec_c, 0.0)) * LOG2E
        bc = _sel_rows(tri_ref[...], gc, 3)
        tc = tc * LOG2E
        smt = smt_ref[ci]
        tr = smt + bifc_ref[...]
        gdec_r = -jnp.exp(acol_ref[...]) * _softplus(smt + dcol_ref[...])
        gr = jnp.where(is_lf_r, _log_sigmoid(tr), jnp.where(is_da_r, gdec_r, 0.0)) * LOG2E
        br = _sel_cols(gr, trit_ref[...], 3)
        tr = tr * LOG2E
        beta_all = pltpu.roll(_sigmoid(sm), SM_DA - SM_BETA, 1)
        eb_all = jnp.exp2(bc)
        beb_all = beta_all * eb_all
        kdec_all = jnp.exp2(bc[L - 1:L, :] - bc)
        gpre = _dot(sm, wlr_ref[...]) + bg_ref[...]
        glog_all = _log_sigmoid(gpre) * (LOG2E / GLA_TAU)

        def mlstm_head(h):
            cols = slice(h * HEAD_DIM, (h + 1) * HEAD_DIM)
            q = pm_ref[rows, h * HEAD_DIM:(h + 1) * HEAD_DIM]
            k = pm_ref[rows, W + h * HEAD_DIM:W + (h + 1) * HEAD_DIM]
            v = pm_ref[rows, 2 * W + h * HEAD_DIM:2 * W + (h + 1) * HEAD_DIM]
            b_c = bc[:, SM_LF + h:SM_LF + h + 1]
            li_c = tc[:, SM_LI + h:SM_LI + h + 1]
            b_r = br[SM_LF + h:SM_LF + h + 1, :]
            li_r = tr[SM_LI + h:SM_LI + h + 1, :]
            m_old = m_sc[h][0:1, 0:1]
            n_old = n_sc[h][0:1, :]
            c_old = c_sc[h]

            s_qk = _dot_nt(q, k)
            q_c = _dot(q, c_old)
            yield
            d = jnp.where(causal, b_c - b_r + li_r, NEG_BIG)
            inter = b_c + m_old
            m_t = jnp.maximum(inter, jnp.max(d, axis=1, keepdims=True))
            p = jnp.exp2(d - m_t) * s_qk
            w_inter = jnp.exp2(inter - m_t)
            b_last = b_c[L - 1:L, :]
            g_s = b_last - b_c + li_c
            m_new = jnp.maximum(b_last + m_old, jnp.max(g_s, axis=0, keepdims=True))
            w_s = jnp.exp2(g_s - m_new)
            w_old = jnp.exp2(b_last + m_old - m_new)
            kw = k.astype(F32) * w_s
            pv = _dot(p, v)
            kv = _dot_tn(kw, v)
            yield
            num = pv + w_inter * q_c
            den = (jnp.sum(p, axis=1, keepdims=True)
                   + w_inter * jnp.sum(q.astype(F32) * n_old, axis=1, keepdims=True))
            hout = num / jnp.maximum(jnp.abs(den), jnp.exp2(-m_t))
            c_sc[h] = w_old * c_old + kv
            n_new = w_old * n_old + jnp.sum(kw, axis=0, keepdims=True)
            n_sc[h] = jnp.broadcast_to(n_new, (SUBLANES, LANES))
            m_sc[h] = jnp.broadcast_to(m_new, (SUBLANES, LANES))
            ogate = pm_ref[rows, 3 * W + h * HEAD_DIM:3 * W + (h + 1) * HEAD_DIM].astype(F32)
            gated = _sigmoid_of_twice(ogate) * hout
            out_ref[rows, cols] = _head_rmsnorm(gated, ghm_ref[:, cols]).astype(out_ref.dtype)

        def gla_head(h):
            cols = slice(h * HEAD_DIM, (h + 1) * HEAD_DIM)
            qb = pg_ref[rows, h * HEAD_DIM:(h + 1) * HEAD_DIM]
            kb = pg_ref[rows, W + h * HEAD_DIM:W + (h + 1) * HEAD_DIM]
            q = qb.astype(F32)
            k = kb.astype(F32)
            v = pg_ref[rows, 2 * W + h * HEAD_DIM:2 * W + (h + 1) * HEAD_DIM]
            e = _sel_rows(segm_ref[...], glog_all[:, cols], 2)
            yield
            bsum = e[nlow * L:(nlow + 1) * L]
            st_old = st_sc[h]
            q_st = _dot_nt(q * jnp.exp2(bsum), st_old)
            vk = _dot_tn(v, k * jnp.exp2(bsum[L - 1:L, :] - bsum))
            yield
            st_sc[h] = jnp.exp2(bsum[L - 1:L, :]) * st_old + vk
            a = jnp.zeros((L, L), F32)
            for lv in range(nlev):
                if lv < nlow:
                    lw = e[lv * L:(lv + 1) * L]
                else:
                    half = 1 << lv
                    b3 = bsum.reshape(L // (2 * half), 2 * half, HEAD_DIM)
                    lw = (b3 - b3[:, half - 1:half, :]).reshape(L, HEAD_DIM) * lvsign_ref[lv - nlow]
                ex = jnp.exp2(lw).astype(BF16)
                a = a + lvmask_ref[lv] * _dot_nt(qb * ex, kb * ex)
                yield
            dqk = jnp.sum(q * k, axis=1, keepdims=True)
            o = _dot(a, v) + dqk * v.astype(F32) + q_st
            yield
            rgate = pg_ref[rows, 3 * W + h * HEAD_DIM:3 * W + (h + 1) * HEAD_DIM].astype(F32)
            hn = _head_rmsnorm(o, ghl_ref[:, cols]) * _silu_of_twice(rgate)
            out_ref[rows, W + h * HEAD_DIM:W + (h + 1) * HEAD_DIM] = hn.astype(out_ref.dtype)

        def gdn_conv(piece, h, l2_mult):
            c0 = piece * W + h * HEAD_DIM
            xb = pd_ref[rows, c0:c0 + HEAD_DIM]
            x = xb.astype(F32)
            w = cw_ref[:, c0:c0 + HEAD_DIM]
            prev = jnp.where(ci == 0, tail_sc[:, c0:c0 + HEAD_DIM].astype(F32),
                             pd_ref[prev_rows, c0:c0 + HEAD_DIM].astype(F32))
            z = jnp.concatenate([prev[BF16_ROWS - SUBLANES:, :], x], axis=0)
            y = x * w[CONV_K - 1:CONV_K, :]
            for dlt in range(1, CONV_K):
                y = y + z[SUBLANES - dlt:SUBLANES - dlt + L, :] * w[CONV_K - 1 - dlt:CONV_K - dlt, :]
            y = _silu_of_twice(y)
            if l2_mult is not None:
                y = y * (lax.rsqrt(jnp.sum(y * y, axis=-1, keepdims=True) + EPS) * l2_mult)
            return y

        def gdn_head(h):
            cols = slice(h * HEAD_DIM, (h + 1) * HEAD_DIM)
            lane = slice(SM_DA + h, SM_DA + h + 1)
            q = gdn_conv(0, h, HEAD_DIM ** -0.5)
            k = gdn_conv(1, h, 1.0)
            v = gdn_conv(2, h, None)
            b_c = bc[:, lane]
            b_r = br[SM_DA + h:SM_DA + h + 1, :]
            beta = beta_all[:, lane]
            s_old = s_sc[h]

            kk = _dot_nt(k, k)
            yield
            decay = jnp.where(causal, jnp.exp2(jnp.minimum(b_c - b_r, 0.0)), 0.0)
            a = (beta * decay) * kk
            n = yield from _unit_lower_inverse_minus_eye(a, blk_masks)
            rhs = jnp.concatenate([beb_all[:, lane] * k, beta * v], axis=1)
            n_rhs = _dot(n, rhs)
            qk_raw = _dot_nt(q, k)
            yield
            wu = rhs + n_rhs
            w, u = wu[:, :HEAD_DIM], wu[:, HEAD_DIM:]
            ws_qs = _dot(jnp.concatenate([w, q * eb_all[:, lane]], axis=0), s_old)
            yield
            v_new = u - ws_qs[:L]
            qk_v = _dot(qk_raw * decay, v_new)
            k_v = _dot_tn(k * kdec_all[:, lane], v_new)
            yield
            o = ws_qs[L:] + qk_v
            s_sc[h] = eb_all[L - 1:L, lane] * s_old + k_v
            zgate = pd_ref[rows, 3 * W + h * HEAD_DIM:3 * W + (h + 1) * HEAD_DIM].astype(F32)
            hn = _head_rmsnorm(o, ghd_ref[:, cols]) * _silu_of_twice(zgate)
            out_ref[rows, 2 * W + h * HEAD_DIM:2 * W + (h + 1) * HEAD_DIM] = hn.astype(out_ref.dtype)

        gens = [f(h) for f in (gdn_head, gla_head, mlstm_head) for h in range(N_HEADS)]
        _round_robin(gens)
        return carry

    lax.fori_loop(0, nchunk, chunk, 0)
    tail_sc[...] = pd_ref[tb - BF16_ROWS:tb, 0:3 * W]


def _mixers(P, sm, smt, consts, B, S, tb):
    T = B * S
    nblk = S // tb
    row_blk = lambda b, c: b * nblk + c
    in_specs = [pl.BlockSpec((tb, GROUP_W), lambda b, c, g=g: (row_blk(b, c), g)) for g in range(3)]
    in_specs.append(pl.BlockSpec((tb, LANES), lambda b, c: (row_blk(b, c), 0)))
    in_specs.append(pl.BlockSpec((tb // CHUNK, SM_USED, CHUNK), lambda b, c: (row_blk(b, c), 0, 0)))
    in_specs += [_full_spec(a.shape) for a in consts]
    state = pltpu.VMEM((N_HEADS, HEAD_DIM, HEAD_DIM), F32)
    small = pltpu.VMEM((N_HEADS, SUBLANES, LANES), F32)
    return pl.pallas_call(
        _mixers_kernel,
        grid=(B, nblk),
        in_specs=in_specs,
        out_specs=pl.BlockSpec((tb, N_BRANCH * MIX_W), lambda b, c: (row_blk(b, c), 0)),
        out_shape=jax.ShapeDtypeStruct((T, N_BRANCH * MIX_W), BF16),
        scratch_shapes=[state, small, small, state, state,
                        pltpu.VMEM((BF16_ROWS, 3 * MIX_W), BF16)],
        compiler_params=pltpu.CompilerParams(
            dimension_semantics=("arbitrary", "arbitrary"), vmem_limit_bytes=VMEM_LIMIT),
        name="mixers",
    )(P, P, P, sm, smt, *consts)


def _merge_kernel(x_ref, h_ref, gates_ref, wup_ref, wout_ref, out_ref):
    d = x_ref.shape[1]
    acc = None
    for n in range(N_BRANCH):
        up = jnp.dot(h_ref[:, n * MIX_W:(n + 1) * MIX_W], wup_ref[n], preferred_element_type=F32)
        t = _sigmoid(gates_ref[:, n * d:(n + 1) * d].astype(F32)) * up
        acc = t if acc is None else acc + t
    out_ref[...] = x_ref[...] + jnp.dot(acc.astype(BF16), wout_ref[...], preferred_element_type=F32)


def _merge(x2, h, P, wup, wout, layer, tm):
    T, D = x2.shape
    gate_blk = (3 * GROUP_W) // (N_BRANCH * D)
    row = lambda i: (i, 0)
    return pl.pallas_call(
        _merge_kernel,
        grid=(T // tm,),
        in_specs=[pl.BlockSpec((tm, D), row), pl.BlockSpec((tm, N_BRANCH * MIX_W), row),
                  pl.BlockSpec((tm, N_BRANCH * D), lambda i: (i, gate_blk)),
                  pl.BlockSpec((None, N_BRANCH, MIX_W, D), lambda i: (layer, 0, 0, 0)),
                  pl.BlockSpec((None, D, D), lambda i: (layer, 0, 0))],
        out_specs=pl.BlockSpec((tm, D), row),
        out_shape=jax.ShapeDtypeStruct((T, D), F32),
        compiler_params=pltpu.CompilerParams(
            dimension_semantics=("arbitrary",), vmem_limit_bytes=VMEM_LIMIT),
        name="merge",
    )(x2, h, P, wup, wout)


def _rmsnorm(x, g_row):
    return x * lax.rsqrt(jnp.mean(x * x, axis=-1, keepdims=True) + EPS) * g_row


def _mlp_kernel(x_ref, g_ref, w1_ref, w2_ref, gout_ref, out_ref, u_sc, acc_sc, *, norm_out):
    f = pl.program_id(1)

    @pl.when(f == 0)
    def _():
        u_sc[...] = _rmsnorm(x_ref[...], g_ref[...]).astype(BF16)
        acc_sc[...] = jnp.zeros_like(acc_sc)

    hmid = jnp.maximum(jnp.dot(u_sc[...], w1_ref[...], preferred_element_type=F32), 0.0)
    acc_sc[...] += jnp.dot((hmid * hmid).astype(BF16), w2_ref[...], preferred_element_type=F32)

    @pl.when(f == pl.num_programs(1) - 1)
    def _():
        y = x_ref[...] + acc_sc[...]
        out_ref[...] = _rmsnorm(y, gout_ref[...]) if norm_out else y


def _mlp(x2, g_row, w1, w2, layer, tm, tf, g_out_row=None):
    T, D = x2.shape
    F = w1.shape[2]
    norm_out = g_out_row is not None
    return pl.pallas_call(
        functools.partial(_mlp_kernel, norm_out=norm_out),
        grid=(T // tm, F // tf),
        in_specs=[pl.BlockSpec((tm, D), lambda i, f: (i, 0)),
                  pl.BlockSpec((1, D), lambda i, f: (0, 0)),
                  pl.BlockSpec((None, D, tf), lambda i, f: (layer, 0, f)),
                  pl.BlockSpec((None, tf, D), lambda i, f: (layer, f, 0)),
                  pl.BlockSpec((1, D), lambda i, f: (0, 0))],
        out_specs=pl.BlockSpec((tm, D), lambda i, f: (i, 0)),
        out_shape=jax.ShapeDtypeStruct((T, D), F32),
        scratch_shapes=[pltpu.VMEM((tm, D), BF16), pltpu.VMEM((tm, D), F32)],
        compiler_params=pltpu.CompilerParams(
            dimension_semantics=("arbitrary", "arbitrary"), vmem_limit_bytes=VMEM_LIMIT),
        name="mlp",
    )(x2, g_row, w1, w2, g_out_row if norm_out else g_row)


def _tile(n, want):
    t = min(n, want)
    while n % t:
        t //= 2
    return t


def _pad_rows(vals, offset, width=LANES):
    out = jnp.zeros((vals.shape[0], 1, width), F32)
    return out.at[:, 0, offset:offset + vals.shape[1]].set(vals.astype(F32))


def _as_cols(rows):
    return jnp.broadcast_to(rows[:, 0, :SM_USED, None], (rows.shape[0], SM_USED, CHUNK))


def kernel(x, w_in, b_if, w_gla_lr, b_gla, conv_gdn, a_log, dt_bias, g_norm_mix, g_norm_mlp,
           g_head_mlstm, g_head_gla, g_head_gdn, w_up, w_out, w_mlp_in, w_mlp_out, g_final):
    B, S, D = x.shape
    depth = w_in.shape[0]
    T = B * S
    H, W = N_HEADS, MIX_W
    assert S % CHUNK == 0 and D == D_MODEL

    tri_np, lvm_np, lvmask_np = _np_consts(CHUNK)
    tri = jnp.asarray(np.tile(tri_np, (1, 3)), BF16)
    trit = jnp.asarray(np.tile(tri_np.T, (3, 1)), BF16)
    n_low = int(np.log2(SUBLANES))
    segm = jnp.asarray(np.tile(np.concatenate(list(lvm_np[:n_low]) + [tri_np], axis=0), (1, 2)), BF16)
    lvmask = jnp.asarray(lvmask_np, F32)
    ti = np.arange(CHUNK)[:, None]
    ui = np.arange(CHUNK)[None, :]
    lvsign = jnp.asarray(np.stack([
        np.broadcast_to(np.where((ti // (1 << lv)) % 2 == 1, 1.0, -1.0), (CHUNK, HEAD_DIM))
        for lv in range(n_low, lvmask_np.shape[0])]), F32)
    blk_list = [((ti // SOLVE_BASE) == (ui // SOLVE_BASE)) & (ui < ti)]
    c = SOLVE_BASE
    while c < CHUNK:
        blk_list.append(((ti // (2 * c)) == (ui // (2 * c))) & ((ti // c) != (ui // c)))
        c *= 2
    blk = jnp.asarray(np.stack(blk_list), F32)

    sizes = [W, W, W, W, H, H, W, W, W, W, GLA_RANK, W, W, W, W, H, H, N_BRANCH * D]
    offs = np.concatenate([[0], np.cumsum(sizes)])
    piece = lambda i: w_in[:, :, offs[i]:offs[i + 1]]
    q_scale = HEAD_DIM ** -0.5
    big_ids = (0, 1, 2, 3, 6, 7, 8, 9, 11, 12, 13, 14, 17)
    pieces, dst = [], 0
    scales = {0: q_scale, 6: q_scale, 3: 0.5, 9: 0.5, 14: 0.5}
    for i in big_ids:
        pieces.append((int(offs[i]), dst, sizes[i], scales.get(i, 1.0)))
        dst += sizes[i]
    w_big = _regroup(w_in, tuple(pieces), N_BIG, 128)
    w_small = jnp.concatenate([piece(i) for i in (4, 5, 10, 15, 16)], axis=2)
    w_sm = jnp.pad(w_small, ((0, 0), (0, 0), (0, LANES - SM_USED))).astype(BF16)
    w_smt = jnp.swapaxes(w_small, 1, 2).astype(BF16)
    w_up_b = w_up.astype(BF16)
    w_out_b = w_out.astype(BF16)
    w_mlp_in_b = w_mlp_in.astype(BF16)
    w_mlp_out_b = w_mlp_out.astype(BF16)

    conv_half = 0.5 * conv_gdn.astype(F32)
    bif_rows = _pad_rows(b_if, SM_LI)
    a_rows = _pad_rows(a_log, SM_DA)
    dt_rows = _pad_rows(dt_bias, SM_DA)
    bif_cols, a_cols, dt_cols = _as_cols(bif_rows), _as_cols(a_rows), _as_cols(dt_rows)
    wlr = jnp.zeros((depth, LANES, W), F32).at[:, SM_LLR:SM_LLR + GLA_RANK].set(w_gla_lr).astype(BF16)

    tm_proj = _tile(T, 1024)
    tn_proj = 3072
    tb = _tile(S, 512)
    tm_merge = _tile(T, 1024)
    tm_mlp = _tile(T, 1024)
    tf_mlp = _tile(w_mlp_in.shape[2], 2048)

    x2 = x.reshape(T, D)
    for l in range(depth):
        P, sm, smt = _proj(x2, g_norm_mix[l].reshape(1, D), w_big, w_sm, w_smt, l,
                           tm_proj, tn_proj)
        consts = (bif_rows[l], bif_cols[l], g_head_mlstm[l].reshape(1, W),
                  wlr[l], b_gla[l].reshape(1, W), g_head_gla[l].reshape(1, W), segm, lvmask, lvsign,
                  conv_half[l], a_rows[l], a_cols[l], dt_rows[l], dt_cols[l],
                  g_head_gdn[l].reshape(1, W), blk, tri, trit)
        h = _mixers(P, sm, smt, consts, B, S, tb)
        x2 = _merge(x2, h, P, w_up_b, w_out_b, l, tm_merge)
        x2 = _mlp(x2, g_norm_mlp[l].reshape(1, D), w_mlp_in_b, w_mlp_out_b, l, tm_mlp, tf_mlp,
                  g_final.reshape(1, D) if l == depth - 1 else None)
    return x2.reshape(B, S, D)
```

```python
import functools

import numpy as np
import jax
import jax.numpy as jnp
from jax import lax
from jax.experimental import pallas as pl
from jax.experimental.pallas import tpu as pltpu

F32 = jnp.float32
BF16 = jnp.bfloat16

N_HEADS = 4
HEAD_DIM = 128
MIX_W = N_HEADS * HEAD_DIM
N_BRANCH = 3
GLA_RANK = 16
GLA_TAU = 16.0
CONV_K = 4
EPS = 1e-6
D_MODEL = 1024

LANES = 128
SUBLANES = 8
BF16_ROWS = 16
CHUNK = 128
SOLVE_BASE = 16
NEG_BIG = -1e30
LOG2E = 1.4426950408889634
VMEM_LIMIT = 56 * 1024 * 1024

SM_LI, SM_LF, SM_LLR, SM_BETA, SM_DA = 0, 4, 8, 24, 28
SM_USED = 32
GROUP_W = 4 * MIX_W
N_BIG = 3 * GROUP_W + N_BRANCH * D_MODEL


def _dot(a, b):
    return jnp.dot(a.astype(BF16), b.astype(BF16), preferred_element_type=F32)


def _dot_nt(a, b):
    return lax.dot_general(a.astype(BF16), b.astype(BF16), (((1,), (1,)), ((), ())),
                           preferred_element_type=F32)


def _dot_tn(a, b):
    return lax.dot_general(a.astype(BF16), b.astype(BF16), (((0,), (0,)), ((), ())),
                           preferred_element_type=F32)


def _split_bf16(x, n):
    parts = []
    r = x
    for i in range(n):
        p = r.astype(BF16)
        parts.append(p)
        if i + 1 < n:
            r = r - p.astype(F32)
    return parts


def _sel_rows(m01_rep, x, n):
    return jnp.dot(m01_rep, jnp.concatenate(_split_bf16(x, n), axis=0), preferred_element_type=F32)


def _sel_cols(x, m01_rep, n):
    return jnp.dot(jnp.concatenate(_split_bf16(x, n), axis=1), m01_rep, preferred_element_type=F32)


def _log_sigmoid(x):
    return jnp.minimum(x, 0.0) - jnp.log(1.0 + jnp.exp(-jnp.abs(x)))


def _softplus(x):
    return jnp.maximum(x, 0.0) + jnp.log(1.0 + jnp.exp(-jnp.abs(x)))


def _sigmoid_of_twice(h):
    return 0.5 * jnp.tanh(h) + 0.5


def _silu_of_twice(h):
    return h + h * jnp.tanh(h)


def _sigmoid(x):
    return _sigmoid_of_twice(0.5 * x)


def _head_rmsnorm(h, g_row):
    return h * lax.rsqrt(jnp.mean(h * h, axis=-1, keepdims=True) + EPS) * g_row


def _round_robin(gens):
    live = list(gens)
    while live:
        nxt = []
        for g in live:
            try:
                next(g)
                nxt.append(g)
            except StopIteration:
                pass
        live = nxt


def _np_consts(L):
    t = np.arange(L)[:, None]
    u = np.arange(L)[None, :]
    tri = (u <= t)
    lv_m, lv_mask = [], []
    c = 1
    while c < L:
        base = (t // (2 * c)) * (2 * c)
        ref = base + c - 1
        second = t >= base + c
        m = np.where(second, (u > ref) & (u <= t), (u > t) & (u <= ref))
        ub = (u // (2 * c)) * (2 * c)
        mask = (base == ub) & second & (u < ub + c)
        lv_m.append(m)
        lv_mask.append(mask)
        c *= 2
    return tri, np.stack(lv_m), np.stack(lv_mask)


def _full_spec(shape):
    nd = len(shape)
    return pl.BlockSpec(shape, lambda *_: (0,) * nd)


def _regroup_kernel(wt_ref, big_ref, small_ref, *, big_pieces, small_pieces):
    for src, dst, n, scale in big_pieces:
        v = wt_ref[src:src + n, :]
        if scale != 1.0:
            v = v * scale
        big_ref[dst:dst + n, :] = v.astype(big_ref.dtype)
    small = [wt_ref[src:src + n, :] for src, n in small_pieces]
    n_small = sum(n for _, n in small_pieces)
    small.append(jnp.zeros((small_ref.shape[0] - n_small, wt_ref.shape[1]), F32))
    small_ref[...] = jnp.concatenate(small, axis=0).astype(small_ref.dtype)


def _regroup(w_in_t, big_pieces, small_pieces, tc):
    depth, n_in, D = w_in_t.shape
    return pl.pallas_call(
        functools.partial(_regroup_kernel, big_pieces=big_pieces, small_pieces=small_pieces),
        grid=(depth, D // tc),
        in_specs=[pl.BlockSpec((None, n_in, tc), lambda l, c: (l, 0, c))],
        out_specs=[pl.BlockSpec((None, N_BIG, tc), lambda l, c: (l, 0, c)),
                   pl.BlockSpec((None, LANES, tc), lambda l, c: (l, 0, c))],
        out_shape=[jax.ShapeDtypeStruct((depth, N_BIG, D), BF16),
                   jax.ShapeDtypeStruct((depth, LANES, D), BF16)],
        compiler_params=pltpu.CompilerParams(
            dimension_semantics=("arbitrary", "arbitrary"), vmem_limit_bytes=VMEM_LIMIT),
        name="regroup",
    )(w_in_t)


_NT_DIMS = (((1,), (1,)), ((), ()))


def _proj_kernel(x_ref, g_ref, wb_ref, ws_ref, p_ref, sm_ref, smt_ref, u_sc):
    j = pl.program_id(1)

    @pl.when(j == 0)
    def _():
        x = x_ref[...]
        u = x * lax.rsqrt(jnp.mean(x * x, axis=-1, keepdims=True) + EPS) * g_ref[...]
        ub = u.astype(BF16)
        u_sc[...] = ub
        sm_ref[...] = lax.dot_general(ub, ws_ref[...], _NT_DIMS, preferred_element_type=F32)
        smt = lax.dot_general(ws_ref[:SM_USED, :], ub, _NT_DIMS, preferred_element_type=F32)
        for c in range(smt_ref.shape[0]):
            smt_ref[c] = smt[:, c * CHUNK:(c + 1) * CHUNK]

    p = lax.dot_general(u_sc[...], wb_ref[...], _NT_DIMS, preferred_element_type=F32)
    p_ref[...] = p.astype(p_ref.dtype)


def _proj(x2, g_row, w_big_t, w_small_t, layer, tm, tn):
    T, D = x2.shape
    return pl.pallas_call(
        _proj_kernel,
        grid=(T // tm, N_BIG // tn),
        in_specs=[
            pl.BlockSpec((tm, D), lambda i, j: (i, 0)),
            pl.BlockSpec((1, D), lambda i, j: (0, 0)),
            pl.BlockSpec((None, tn, D), lambda i, j: (layer, j, 0)),
            pl.BlockSpec((None, LANES, D), lambda i, j: (layer, 0, 0)),
        ],
        out_specs=[
            pl.BlockSpec((tm, tn), lambda i, j: (i, j)),
            pl.BlockSpec((tm, LANES), lambda i, j: (i, 0)),
            pl.BlockSpec((tm // CHUNK, SM_USED, CHUNK), lambda i, j: (i, 0, 0)),
        ],
        out_shape=[
            jax.ShapeDtypeStruct((T, N_BIG), BF16),
            jax.ShapeDtypeStruct((T, LANES), F32),
            jax.ShapeDtypeStruct((T // CHUNK, SM_USED, CHUNK), F32),
        ],
        scratch_shapes=[pltpu.VMEM((tm, D), BF16)],
        compiler_params=pltpu.CompilerParams(
            dimension_semantics=("arbitrary", "arbitrary"), vmem_limit_bytes=VMEM_LIMIT),
        name="proj",
    )(x2, g_row, w_big_t, w_small_t)


def _unit_lower_inverse_minus_eye(a, blk_masks):
    a0 = a * blk_masks[0]
    n = -a0
    pw = _dot(a0, a0)
    yield
    span = 2
    while span < SOLVE_BASE:
        n_pw = _dot(n, pw)
        pw_next = _dot(pw, pw) if 2 * span < SOLVE_BASE else None
        yield
        n = n + pw + n_pw
        pw = pw_next
        span *= 2
    for m in blk_masks[1:]:
        c = a * m
        n_c = _dot(n, c)
        yield
        t = c + n_c
        t_n = _dot(t, n)
        yield
        n = n - t - t_n
    return n


def _mixers_kernel(pm_ref, pg_ref, pd_ref, sm_ref, smt_ref,
                   bifr_ref, bifc_ref, ghm_ref,
                   wlr_ref, bg_ref, ghl_ref, segm_ref, lvmask_ref, lvsign_ref,
                   cw_ref, arow_ref, acol_ref, drow_ref, dcol_ref, ghd_ref, blk_ref,
                   tri_ref, trit_ref,
                   out_ref,
                   c_sc, n_sc, m_sc, st_sc, s_sc, tail_sc):
    L = CHUNK
    W = MIX_W
    tb = pm_ref.shape[0]
    nchunk = tb // L
    nlev = lvmask_ref.shape[0]
    nlow = segm_ref.shape[0] // L - 1
    blk_masks = [blk_ref[i] for i in range(blk_ref.shape[0])]

    @pl.when(pl.program_id(1) == 0)
    def _():
        c_sc[...] = jnp.zeros_like(c_sc)
        n_sc[...] = jnp.zeros_like(n_sc)
        m_sc[...] = jnp.zeros_like(m_sc)
        st_sc[...] = jnp.zeros_like(st_sc)
        s_sc[...] = jnp.zeros_like(s_sc)
        tail_sc[...] = jnp.zeros_like(tail_sc)

    row_i = lax.broadcasted_iota(jnp.int32, (L, L), 0)
    col_i = lax.broadcasted_iota(jnp.int32, (L, L), 1)
    causal = col_i <= row_i
    sm_col = lax.broadcasted_iota(jnp.int32, (L, LANES), 1)
    is_lf_c = (sm_col >= SM_LF) & (sm_col < SM_LF + N_HEADS)
    is_da_c = (sm_col >= SM_DA) & (sm_col < SM_DA + N_HEADS)
    smt_row = lax.broadcasted_iota(jnp.int32, (SM_USED, L), 0)
    is_lf_r = (smt_row >= SM_LF) & (smt_row < SM_LF + N_HEADS)
    is_da_r = (smt_row >= SM_DA) & (smt_row < SM_DA + N_HEADS)

    def chunk(ci, carry):
        r0 = pl.multiple_of(ci * L, L)
        rows = pl.ds(r0, L)
        prev_rows = pl.ds(pl.multiple_of(jnp.maximum(r0 - BF16_ROWS, 0), BF16_ROWS), BF16_ROWS)

        sm = sm_ref[rows, :]
        tc = sm + bifr_ref[...]
        gdec_c = -jnp.exp(arow_ref[...]) * _softplus(sm + drow_ref[...])
        gc = jnp.where(is_lf_c, _log_sigmoid(tc), jnp.where(is_da_c, gdec_c, 0.0)) * LOG2E
        bc = _sel_rows(tri_ref[...], gc, 3)
        tc = tc * LOG2E
        smt = smt_ref[ci]
        tr = smt + bifc_ref[...]
        gdec_r = -jnp.exp(acol_ref[...]) * _softplus(smt + dcol_ref[...])
        gr = jnp.where(is_lf_r, _log_sigmoid(tr), jnp.where(is_da_r, gdec_r, 0.0)) * LOG2E
        br = _sel_cols(gr, trit_ref[...], 3)
        tr = tr * LOG2E
        beta_all = pltpu.roll(_sigmoid(sm), SM_DA - SM_BETA, 1)
        eb_all = jnp.exp2(bc)
        beb_all = beta_all * eb_all
        kdec_all = jnp.exp2(bc[L - 1:L, :] - bc)
        gpre = _dot(sm, wlr_ref[...]) + bg_ref[...]
        glog_all = _log_sigmoid(gpre) * (LOG2E / GLA_TAU)

        def mlstm_head(h):
            cols = slice(h * HEAD_DIM, (h + 1) * HEAD_DIM)
            q = pm_ref[rows, h * HEAD_DIM:(h + 1) * HEAD_DIM]
            k = pm_ref[rows, W + h * HEAD_DIM:W + (h + 1) * HEAD_DIM]
            v = pm_ref[rows, 2 * W + h * HEAD_DIM:2 * W + (h + 1) * HEAD_DIM]
            b_c = bc[:, SM_LF + h:SM_LF + h + 1]
            li_c = tc[:, SM_LI + h:SM_LI + h + 1]
            b_r = br[SM_LF + h:SM_LF + h + 1, :]
            li_r = tr[SM_LI + h:SM_LI + h + 1, :]
            m_old = m_sc[h][0:1, 0:1]
            n_old = n_sc[h][0:1, :]
            c_old = c_sc[h]

            s_qk = _dot_nt(q, k)
            q_c = _dot(q, c_old)
            yield
            d = jnp.where(causal, b_c + (li_r - b_r), NEG_BIG)
            inter = b_c + m_old
            m_t = jnp.maximum(inter, jnp.max(d, axis=1, keepdims=True))
            p = jnp.exp2(d - m_t) * s_qk
            w_inter = jnp.exp2(inter - m_t)
            b_last = b_c[L - 1:L, :]
            g_s = b_last - b_c + li_c
            m_new = jnp.maximum(b_last + m_old, jnp.max(g_s, axis=0, keepdims=True))
            w_s = jnp.exp2(g_s - m_new)
            w_old = jnp.exp2(b_last + m_old - m_new)
            kw = k.astype(F32) * w_s
            pv = _dot(p, v)
            kv = _dot_tn(kw, v)
            yield
            num = pv + w_inter * q_c
            den = (jnp.sum(p, axis=1, keepdims=True)
                   + w_inter * jnp.sum(q.astype(F32) * n_old, axis=1, keepdims=True))
            hout = num / jnp.maximum(jnp.abs(den), jnp.exp2(-m_t))
            c_sc[h] = w_old * c_old + kv
            n_new = w_old * n_old + jnp.sum(kw, axis=0, keepdims=True)
            n_sc[h] = jnp.broadcast_to(n_new, (SUBLANES, LANES))
            m_sc[h] = jnp.broadcast_to(m_new, (SUBLANES, LANES))
            ogate = pm_ref[rows, 3 * W + h * HEAD_DIM:3 * W + (h + 1) * HEAD_DIM].astype(F32)
            gated = _sigmoid_of_twice(ogate) * hout
            out_ref[rows, cols] = _head_rmsnorm(gated, ghm_ref[:, cols]).astype(out_ref.dtype)

        def gla_head(h):
            cols = slice(h * HEAD_DIM, (h + 1) * HEAD_DIM)
            qb = pg_ref[rows, h * HEAD_DIM:(h + 1) * HEAD_DIM]
            kb = pg_ref[rows, W + h * HEAD_DIM:W + (h + 1) * HEAD_DIM]
            q = qb.astype(F32)
            k = kb.astype(F32)
            v = pg_ref[rows, 2 * W + h * HEAD_DIM:2 * W + (h + 1) * HEAD_DIM]
            e = _sel_rows(segm_ref[...], glog_all[:, cols], 2)
            yield
            bsum = e[nlow * L:(nlow + 1) * L]
            st_old = st_sc[h]
            q_st = _dot_nt(q * jnp.exp2(bsum), st_old)
            vk = _dot_tn(v, k * jnp.exp2(bsum[L - 1:L, :] - bsum))
            yield
            st_sc[h] = jnp.exp2(bsum[L - 1:L, :]) * st_old + vk
            a = None
            for lv in range(nlev):
                if lv < nlow:
                    lw = e[lv * L:(lv + 1) * L]
                else:
                    half = 1 << lv
                    b3 = bsum.reshape(L // (2 * half), 2 * half, HEAD_DIM)
                    lw = (b3 - b3[:, half - 1:half, :]).reshape(L, HEAD_DIM) * lvsign_ref[lv - nlow]
                ex = jnp.exp2(lw).astype(BF16)
                a_lv = lvmask_ref[lv] * _dot_nt(qb * ex, kb * ex)
                a = a_lv if a is None else a + a_lv
                yield
            dqk = jnp.sum(q * k, axis=1, keepdims=True)
            o = _dot(a, v) + dqk * v.astype(F32) + q_st
            yield
            rgate = pg_ref[rows, 3 * W + h * HEAD_DIM:3 * W + (h + 1) * HEAD_DIM].astype(F32)
            hn = _head_rmsnorm(o, ghl_ref[:, cols]) * _silu_of_twice(rgate)
            out_ref[rows, W + h * HEAD_DIM:W + (h + 1) * HEAD_DIM] = hn.astype(out_ref.dtype)

        def gdn_conv(piece, h, l2_mult):
            c0 = piece * W + h * HEAD_DIM
            xb = pd_ref[rows, c0:c0 + HEAD_DIM]
            x = xb.astype(F32)
            w = cw_ref[:, c0:c0 + HEAD_DIM]
            prev = jnp.where(ci == 0, tail_sc[:, c0:c0 + HEAD_DIM].astype(F32),
                             pd_ref[prev_rows, c0:c0 + HEAD_DIM].astype(F32))
            z = jnp.concatenate([prev[BF16_ROWS - SUBLANES:, :], x], axis=0)
            y = x * w[CONV_K - 1:CONV_K, :]
            for dlt in range(1, CONV_K):
                y = y + z[SUBLANES - dlt:SUBLANES - dlt + L, :] * w[CONV_K - 1 - dlt:CONV_K - dlt, :]
            y = _silu_of_twice(y)
            if l2_mult is not None:
                y = y * (lax.rsqrt(jnp.sum(y * y, axis=-1, keepdims=True) + EPS) * l2_mult)
            return y

        def gdn_head(h):
            cols = slice(h * HEAD_DIM, (h + 1) * HEAD_DIM)
            lane = slice(SM_DA + h, SM_DA + h + 1)
            q = gdn_conv(0, h, HEAD_DIM ** -0.5)
            k = gdn_conv(1, h, 1.0)
            v = gdn_conv(2, h, None)
            b_c = bc[:, lane]
            b_r = br[SM_DA + h:SM_DA + h + 1, :]
            beta = beta_all[:, lane]
            s_old = s_sc[h]

            kk = _dot_nt(k, k)
            yield
            decay = jnp.where(causal, jnp.exp2(jnp.minimum(b_c - b_r, 0.0)), 0.0)
            a = (beta * decay) * kk
            n = yield from _unit_lower_inverse_minus_eye(a, blk_masks)
            rhs = jnp.concatenate([beb_all[:, lane] * k, beta * v], axis=1)
            n_rhs = _dot(n, rhs)
            qk_raw = _dot_nt(q, k)
            yield
            wu = rhs + n_rhs
            w, u = wu[:, :HEAD_DIM], wu[:, HEAD_DIM:]
            ws_qs = _dot(jnp.concatenate([w, q * eb_all[:, lane]], axis=0), s_old)
            yield
            v_new = u - ws_qs[:L]
            qk_v = _dot(qk_raw * decay, v_new)
            k_v = _dot_tn(k * kdec_all[:, lane], v_new)
            yield
            o = ws_qs[L:] + qk_v
            s_sc[h] = eb_all[L - 1:L, lane] * s_old + k_v
            zgate = pd_ref[rows, 3 * W + h * HEAD_DIM:3 * W + (h + 1) * HEAD_DIM].astype(F32)
            hn = _head_rmsnorm(o, ghd_ref[:, cols]) * _silu_of_twice(zgate)
            out_ref[rows, 2 * W + h * HEAD_DIM:2 * W + (h + 1) * HEAD_DIM] = hn.astype(out_ref.dtype)

        gens = [f(h) for f in (gdn_head, gla_head, mlstm_head) for h in range(N_HEADS)]
        _round_robin(gens)
        return carry

    lax.fori_loop(0, nchunk, chunk, 0)
    tail_sc[...] = pd_ref[tb - BF16_ROWS:tb, 0:3 * W]


def _mixers(P, sm, smt, consts, B, S, tb):
    T = B * S
    nblk = S // tb
    row_blk = lambda b, c: b * nblk + c
    in_specs = [pl.BlockSpec((tb, GROUP_W), lambda b, c, g=g: (row_blk(b, c), g)) for g in range(3)]
    in_specs.append(pl.BlockSpec((tb, LANES), lambda b, c: (row_blk(b, c), 0)))
    in_specs.append(pl.BlockSpec((tb // CHUNK, SM_USED, CHUNK), lambda b, c: (row_blk(b, c), 0, 0)))
    in_specs += [_full_spec(a.shape) for a in consts]
    state = pltpu.VMEM((N_HEADS, HEAD_DIM, HEAD_DIM), F32)
    small = pltpu.VMEM((N_HEADS, SUBLANES, LANES), F32)
    return pl.pallas_call(
        _mixers_kernel,
        grid=(B, nblk),
        in_specs=in_specs,
        out_specs=pl.BlockSpec((tb, N_BRANCH * MIX_W), lambda b, c: (row_blk(b, c), 0)),
        out_shape=jax.ShapeDtypeStruct((T, N_BRANCH * MIX_W), BF16),
        scratch_shapes=[state, small, small, state, state,
                        pltpu.VMEM((BF16_ROWS, 3 * MIX_W), BF16)],
        compiler_params=pltpu.CompilerParams(
            dimension_semantics=("arbitrary", "arbitrary"), vmem_limit_bytes=VMEM_LIMIT),
        name="mixers",
    )(P, P, P, sm, smt, *consts)


def _merge_kernel(x_ref, h_ref, gates_ref, wup_ref, wout_ref, out_ref):
    d = x_ref.shape[1]
    acc = None
    for n in range(N_BRANCH):
        up = jnp.dot(h_ref[:, n * MIX_W:(n + 1) * MIX_W], wup_ref[n], preferred_element_type=F32)
        t = _sigmoid(gates_ref[:, n * d:(n + 1) * d].astype(F32)) * up
        acc = t if acc is None else acc + t
    out_ref[...] = x_ref[...] + jnp.dot(acc.astype(BF16), wout_ref[...], preferred_element_type=F32)


def _merge(x2, h, P, wup, wout, layer, tm):
    T, D = x2.shape
    gate_blk = (3 * GROUP_W) // (N_BRANCH * D)
    row = lambda i: (i, 0)
    return pl.pallas_call(
        _merge_kernel,
        grid=(T // tm,),
        in_specs=[pl.BlockSpec((tm, D), row), pl.BlockSpec((tm, N_BRANCH * MIX_W), row),
                  pl.BlockSpec((tm, N_BRANCH * D), lambda i: (i, gate_blk)),
                  pl.BlockSpec((None, N_BRANCH, MIX_W, D), lambda i: (layer, 0, 0, 0)),
                  pl.BlockSpec((None, D, D), lambda i: (layer, 0, 0))],
        out_specs=pl.BlockSpec((tm, D), row),
        out_shape=jax.ShapeDtypeStruct((T, D), F32),
        compiler_params=pltpu.CompilerParams(
            dimension_semantics=("arbitrary",), vmem_limit_bytes=VMEM_LIMIT),
        name="merge",
    )(x2, h, P, wup, wout)


def _rmsnorm(x, g_row):
    return x * lax.rsqrt(jnp.mean(x * x, axis=-1, keepdims=True) + EPS) * g_row


def _mlp_kernel(x_ref, g_ref, w1_ref, w2_ref, gout_ref, out_ref, u_sc, acc_sc, *, norm_out):
    f = pl.program_id(1)

    @pl.when(f == 0)
    def _():
        u_sc[...] = _rmsnorm(x_ref[...], g_ref[...]).astype(BF16)
        acc_sc[...] = jnp.zeros_like(acc_sc)

    hmid = jnp.maximum(jnp.dot(u_sc[...], w1_ref[...], preferred_element_type=F32), 0.0)
    acc_sc[...] += jnp.dot((hmid * hmid).astype(BF16), w2_ref[...], preferred_element_type=F32)

    @pl.when(f == pl.num_programs(1) - 1)
    def _():
        y = x_ref[...] + acc_sc[...]
        out_ref[...] = _rmsnorm(y, gout_ref[...]) if norm_out else y


def _mlp(x2, g_row, w1, w2, layer, tm, tf, g_out_row=None):
    T, D = x2.shape
    F = w1.shape[2]
    norm_out = g_out_row is not None
    return pl.pallas_call(
        functools.partial(_mlp_kernel, norm_out=norm_out),
        grid=(T // tm, F // tf),
        in_specs=[pl.BlockSpec((tm, D), lambda i, f: (i, 0)),
                  pl.BlockSpec((1, D), lambda i, f: (0, 0)),
                  pl.BlockSpec((None, D, tf), lambda i, f: (layer, 0, f)),
                  pl.BlockSpec((None, tf, D), lambda i, f: (layer, f, 0)),
                  pl.BlockSpec((1, D), lambda i, f: (0, 0))],
        out_specs=pl.BlockSpec((tm, D), lambda i, f: (i, 0)),
        out_shape=jax.ShapeDtypeStruct((T, D), F32),
        scratch_shapes=[pltpu.VMEM((tm, D), BF16), pltpu.VMEM((tm, D), F32)],
        compiler_params=pltpu.CompilerParams(
            dimension_semantics=("arbitrary", "arbitrary"), vmem_limit_bytes=VMEM_LIMIT),
        name="mlp",
    )(x2, g_row, w1, w2, g_out_row if norm_out else g_row)


def _tile(n, want):
    t = min(n, want)
    while n % t:
        t //= 2
    return t


def _pad_rows(vals, offset, width=LANES):
    out = jnp.zeros((vals.shape[0], 1, width), F32)
    return out.at[:, 0, offset:offset + vals.shape[1]].set(vals.astype(F32))


def _as_cols(rows):
    return jnp.broadcast_to(rows[:, 0, :SM_USED, None], (rows.shape[0], SM_USED, CHUNK))


def kernel(x, w_in, b_if, w_gla_lr, b_gla, conv_gdn, a_log, dt_bias, g_norm_mix, g_norm_mlp,
           g_head_mlstm, g_head_gla, g_head_gdn, w_up, w_out, w_mlp_in, w_mlp_out, g_final):
    B, S, D = x.shape
    depth = w_in.shape[0]
    T = B * S
    H, W = N_HEADS, MIX_W
    assert S % CHUNK == 0 and D == D_MODEL

    tri_np, lvm_np, lvmask_np = _np_consts(CHUNK)
    tri = jnp.asarray(np.tile(tri_np, (1, 3)), BF16)
    trit = jnp.asarray(np.tile(tri_np.T, (3, 1)), BF16)
    n_low = int(np.log2(SUBLANES))
    segm = jnp.asarray(np.tile(np.concatenate(list(lvm_np[:n_low]) + [tri_np], axis=0), (1, 2)), BF16)
    lvmask = jnp.asarray(lvmask_np, F32)
    ti = np.arange(CHUNK)[:, None]
    ui = np.arange(CHUNK)[None, :]
    lvsign = jnp.asarray(np.stack([
        np.broadcast_to(np.where((ti // (1 << lv)) % 2 == 1, 1.0, -1.0), (CHUNK, HEAD_DIM))
        for lv in range(n_low, lvmask_np.shape[0])]), F32)
    blk_list = [((ti // SOLVE_BASE) == (ui // SOLVE_BASE)) & (ui < ti)]
    c = SOLVE_BASE
    while c < CHUNK:
        blk_list.append(((ti // (2 * c)) == (ui // (2 * c))) & ((ti // c) != (ui // c)))
        c *= 2
    blk = jnp.asarray(np.stack(blk_list), F32)

    sizes = [W, W, W, W, H, H, W, W, W, W, GLA_RANK, W, W, W, W, H, H, N_BRANCH * D]
    offs = [int(o) for o in np.concatenate([[0], np.cumsum(sizes)])]
    q_scale = HEAD_DIM ** -0.5
    big_ids = (0, 1, 2, 3, 6, 7, 8, 9, 11, 12, 13, 14, 17)
    pieces, dst = [], 0
    scales = {0: q_scale, 6: q_scale, 3: 0.5, 9: 0.5, 14: 0.5}
    for i in big_ids:
        pieces.append((offs[i], dst, sizes[i], scales.get(i, 1.0)))
        dst += sizes[i]
    small_pieces = ((offs[4], 2 * H), (offs[10], GLA_RANK), (offs[15], 2 * H))
    assert all(p[0] % SUBLANES == 0 for p in pieces) and all(p[0] % SUBLANES == 0 for p in small_pieces)
    w_big, w_small = _regroup(jnp.swapaxes(w_in, 1, 2), tuple(pieces), small_pieces, LANES)
    w_up_b = w_up.astype(BF16)
    w_out_b = w_out.astype(BF16)
    w_mlp_in_b = w_mlp_in.astype(BF16)
    w_mlp_out_b = w_mlp_out.astype(BF16)

    conv_half = 0.5 * conv_gdn.astype(F32)
    bif_rows = _pad_rows(b_if, SM_LI)
    a_rows = _pad_rows(a_log, SM_DA)
    dt_rows = _pad_rows(dt_bias, SM_DA)
    bif_cols, a_cols, dt_cols = _as_cols(bif_rows), _as_cols(a_rows), _as_cols(dt_rows)
    wlr = jnp.zeros((depth, LANES, W), F32).at[:, SM_LLR:SM_LLR + GLA_RANK].set(w_gla_lr).astype(BF16)

    tm_proj = _tile(T, 1024)
    tn_proj = 3072
    tb = _tile(S, 1024)
    tm_merge = _tile(T, 1024)
    tm_mlp = _tile(T, 1024)
    tf_mlp = _tile(w_mlp_in.shape[2], 2048)

    x2 = x.reshape(T, D)
    for l in range(depth):
        P, sm, smt = _proj(x2, g_norm_mix[l].reshape(1, D), w_big, w_small, l, tm_proj, tn_proj)
        consts = (bif_rows[l], bif_cols[l], g_head_mlstm[l].reshape(1, W),
                  wlr[l], b_gla[l].reshape(1, W), g_head_gla[l].reshape(1, W), segm, lvmask, lvsign,
                  conv_half[l], a_rows[l], a_cols[l], dt_rows[l], dt_cols[l],
                  g_head_gdn[l].reshape(1, W), blk, tri, trit)
        h = _mixers(P, sm, smt, consts, B, S, tb)
        x2 = _merge(x2, h, P, w_up_b, w_out_b, l, tm_merge)
        x2 = _mlp(x2, g_norm_mlp[l].reshape(1, D), w_mlp_in_b, w_mlp_out_b, l, tm_mlp, tf_mlp,
                  g_final.reshape(1, D) if l == depth - 1 else None)
    return x2.reshape(B, S, D)
```

```python
import functools

import numpy as np
import jax
import jax.numpy as jnp
from jax import lax
from jax.experimental import pallas as pl
from jax.experimental.pallas import tpu as pltpu

F32 = jnp.float32
BF16 = jnp.bfloat16

N_HEADS = 4
HEAD_DIM = 128
MIX_W = N_HEADS * HEAD_DIM
N_BRANCH = 3
GLA_RANK = 16
GLA_TAU = 16.0
CONV_K = 4
EPS = 1e-6
D_MODEL = 1024

LANES = 128
SUBLANES = 8
BF16_ROWS = 16
CHUNK = 128
SOLVE_BASE = 16
NEG_BIG = -1e30
LOG2E = 1.4426950408889634
VMEM_LIMIT = 56 * 1024 * 1024

SM_LI, SM_LF, SM_LLR, SM_BETA, SM_DA = 0, 4, 8, 24, 28
SM_USED = 32
GROUP_W = 4 * MIX_W
N_BIG = 3 * GROUP_W + N_BRANCH * D_MODEL


def _dot(a, b):
    return jnp.dot(a.astype(BF16), b.astype(BF16), preferred_element_type=F32)


def _dot_nt(a, b):
    return lax.dot_general(a.astype(BF16), b.astype(BF16), (((1,), (1,)), ((), ())),
                           preferred_element_type=F32)


def _dot_tn(a, b):
    return lax.dot_general(a.astype(BF16), b.astype(BF16), (((0,), (0,)), ((), ())),
                           preferred_element_type=F32)


def _split_bf16(x, n):
    parts = []
    r = x
    for i in range(n):
        p = r.astype(BF16)
        parts.append(p)
        if i + 1 < n:
            r = r - p.astype(F32)
    return parts


def _sel_rows(m01_rep, x, n):
    return jnp.dot(m01_rep, jnp.concatenate(_split_bf16(x, n), axis=0), preferred_element_type=F32)


def _sel_cols(x, m01_rep, n):
    return jnp.dot(jnp.concatenate(_split_bf16(x, n), axis=1), m01_rep, preferred_element_type=F32)


def _log_sigmoid(x):
    return jnp.minimum(x, 0.0) - jnp.log(1.0 + jnp.exp(-jnp.abs(x)))


def _softplus(x):
    return jnp.maximum(x, 0.0) + jnp.log(1.0 + jnp.exp(-jnp.abs(x)))


def _sigmoid_of_twice(h):
    return 0.5 * jnp.tanh(h) + 0.5


def _silu_of_twice(h):
    return h + h * jnp.tanh(h)


def _sigmoid(x):
    return _sigmoid_of_twice(0.5 * x)


def _head_rmsnorm(h, g_row):
    return h * lax.rsqrt(jnp.mean(h * h, axis=-1, keepdims=True) + EPS) * g_row


def _round_robin(gens):
    live = list(gens)
    while live:
        nxt = []
        for g in live:
            try:
                next(g)
                nxt.append(g)
            except StopIteration:
                pass
        live = nxt


def _np_consts(L):
    t = np.arange(L)[:, None]
    u = np.arange(L)[None, :]
    tri = (u <= t)
    lv_m, lv_mask = [], []
    c = 1
    while c < L:
        base = (t // (2 * c)) * (2 * c)
        ref = base + c - 1
        second = t >= base + c
        m = np.where(second, (u > ref) & (u <= t), (u > t) & (u <= ref))
        ub = (u // (2 * c)) * (2 * c)
        mask = (base == ub) & second & (u < ub + c)
        lv_m.append(m)
        lv_mask.append(mask)
        c *= 2
    return tri, np.stack(lv_m), np.stack(lv_mask)


def _full_spec(shape):
    nd = len(shape)
    return pl.BlockSpec(shape, lambda *_: (0,) * nd)


def _regroup_kernel(wt_ref, big_ref, small_ref, *, big_pieces, small_pieces):
    for src, dst, n, scale in big_pieces:
        v = wt_ref[src:src + n, :]
        if scale != 1.0:
            v = v * scale
        big_ref[dst:dst + n, :] = v.astype(big_ref.dtype)
    small = [wt_ref[src:src + n, :] for src, n in small_pieces]
    n_small = sum(n for _, n in small_pieces)
    small.append(jnp.zeros((small_ref.shape[0] - n_small, wt_ref.shape[1]), F32))
    small_ref[...] = jnp.concatenate(small, axis=0).astype(small_ref.dtype)


def _regroup(w_in_t, big_pieces, small_pieces, tc):
    depth, n_in, D = w_in_t.shape
    return pl.pallas_call(
        functools.partial(_regroup_kernel, big_pieces=big_pieces, small_pieces=small_pieces),
        grid=(depth, D // tc),
        in_specs=[pl.BlockSpec((None, n_in, tc), lambda l, c: (l, 0, c))],
        out_specs=[pl.BlockSpec((None, N_BIG, tc), lambda l, c: (l, 0, c)),
                   pl.BlockSpec((None, LANES, tc), lambda l, c: (l, 0, c))],
        out_shape=[jax.ShapeDtypeStruct((depth, N_BIG, D), BF16),
                   jax.ShapeDtypeStruct((depth, LANES, D), BF16)],
        compiler_params=pltpu.CompilerParams(
            dimension_semantics=("arbitrary", "arbitrary"), vmem_limit_bytes=VMEM_LIMIT),
        name="regroup",
    )(w_in_t)


_NT_DIMS = (((1,), (1,)), ((), ()))


def _proj_kernel(x_ref, g_ref, wb_ref, ws_ref, *rest, n_cast):
    cast_in = rest[:n_cast]
    p_ref, sm_ref, smt_ref = rest[n_cast:n_cast + 3]
    cast_out = rest[n_cast + 3:2 * n_cast + 3]
    u_sc = rest[2 * n_cast + 3]
    j = pl.program_id(1)

    @pl.when(j == 0)
    def _():
        x = x_ref[...]
        u = x * lax.rsqrt(jnp.mean(x * x, axis=-1, keepdims=True) + EPS) * g_ref[...]
        ub = u.astype(BF16)
        u_sc[...] = ub
        sm_ref[...] = lax.dot_general(ub, ws_ref[...], _NT_DIMS, preferred_element_type=F32)
        smt = lax.dot_general(ws_ref[:SM_USED, :], ub, _NT_DIMS, preferred_element_type=F32)
        for c in range(smt_ref.shape[0]):
            smt_ref[c] = smt[:, c * CHUNK:(c + 1) * CHUNK]
        for src, dst in zip(cast_in, cast_out):
            dst[...] = src[...].astype(dst.dtype)

    p = lax.dot_general(u_sc[...], wb_ref[...], _NT_DIMS, preferred_element_type=F32)
    p_ref[...] = p.astype(p_ref.dtype)


def _proj(x2, g_row, w_big_t, w_small_t, to_cast, layer, tm, tn):
    T, D = x2.shape
    n_row = T // tm
    cast_in_specs, cast_out_specs, cast_out_shapes = [], [], []
    for w in to_cast:
        _, R, C = w.shape
        assert R % (n_row * BF16_ROWS) == 0
        cast_in_specs.append(pl.BlockSpec((None, R // n_row, C), lambda i, j: (layer, i, 0)))
        cast_out_specs.append(pl.BlockSpec((R // n_row, C), lambda i, j: (i, 0)))
        cast_out_shapes.append(jax.ShapeDtypeStruct((R, C), BF16))
    return pl.pallas_call(
        functools.partial(_proj_kernel, n_cast=len(to_cast)),
        grid=(n_row, N_BIG // tn),
        in_specs=[
            pl.BlockSpec((tm, D), lambda i, j: (i, 0)),
            pl.BlockSpec((1, D), lambda i, j: (0, 0)),
            pl.BlockSpec((None, tn, D), lambda i, j: (layer, j, 0)),
            pl.BlockSpec((None, LANES, D), lambda i, j: (layer, 0, 0)),
        ] + cast_in_specs,
        out_specs=[
            pl.BlockSpec((tm, tn), lambda i, j: (i, j)),
            pl.BlockSpec((tm, LANES), lambda i, j: (i, 0)),
            pl.BlockSpec((tm // CHUNK, SM_USED, CHUNK), lambda i, j: (i, 0, 0)),
        ] + cast_out_specs,
        out_shape=[
            jax.ShapeDtypeStruct((T, N_BIG), BF16),
            jax.ShapeDtypeStruct((T, LANES), F32),
            jax.ShapeDtypeStruct((T // CHUNK, SM_USED, CHUNK), F32),
        ] + cast_out_shapes,
        scratch_shapes=[pltpu.VMEM((tm, D), BF16)],
        compiler_params=pltpu.CompilerParams(
            dimension_semantics=("arbitrary", "arbitrary"), vmem_limit_bytes=VMEM_LIMIT),
        name="proj",
    )(x2, g_row, w_big_t, w_small_t, *to_cast)


def _unit_lower_inverse_minus_eye(a, blk_masks):
    a0 = a * blk_masks[0]
    n = -a0
    pw = _dot(a0, a0)
    yield
    span = 2
    while span < SOLVE_BASE:
        n_pw = _dot(n, pw)
        pw_next = _dot(pw, pw) if 2 * span < SOLVE_BASE else None
        yield
        n = n + pw + n_pw
        pw = pw_next
        span *= 2
    for m in blk_masks[1:]:
        c = a * m
        n_c = _dot(n, c)
        yield
        t = c + n_c
        t_n = _dot(t, n)
        yield
        n = n - t - t_n
    return n


def _mixers_kernel(pm_ref, pg_ref, pd_ref, sm_ref, smt_ref,
                   bifr_ref, bifc_ref, ghm_ref,
                   wlr_ref, bg_ref, ghl_ref, segm_ref, lvmask_ref, lvsign_ref,
                   cw_ref, arow_ref, acol_ref, drow_ref, dcol_ref, ghd_ref, blk_ref,
                   tri_ref, trit_ref,
                   out_ref,
                   c_sc, n_sc, m_sc, st_sc, s_sc, tail_sc):
    L = CHUNK
    W = MIX_W
    tb = pm_ref.shape[0]
    nchunk = tb // L
    nlev = lvmask_ref.shape[0]
    nlow = segm_ref.shape[0] // L - 1
    blk_masks = [blk_ref[i] for i in range(blk_ref.shape[0])]

    @pl.when(pl.program_id(1) == 0)
    def _():
        c_sc[...] = jnp.zeros_like(c_sc)
        n_sc[...] = jnp.zeros_like(n_sc)
        m_sc[...] = jnp.zeros_like(m_sc)
        st_sc[...] = jnp.zeros_like(st_sc)
        s_sc[...] = jnp.zeros_like(s_sc)
        tail_sc[...] = jnp.zeros_like(tail_sc)

    row_i = lax.broadcasted_iota(jnp.int32, (L, L), 0)
    col_i = lax.broadcasted_iota(jnp.int32, (L, L), 1)
    causal = col_i <= row_i
    sm_col = lax.broadcasted_iota(jnp.int32, (L, LANES), 1)
    is_lf_c = (sm_col >= SM_LF) & (sm_col < SM_LF + N_HEADS)
    is_da_c = (sm_col >= SM_DA) & (sm_col < SM_DA + N_HEADS)
    smt_row = lax.broadcasted_iota(jnp.int32, (SM_USED, L), 0)
    is_lf_r = (smt_row >= SM_LF) & (smt_row < SM_LF + N_HEADS)
    is_da_r = (smt_row >= SM_DA) & (smt_row < SM_DA + N_HEADS)

    def chunk(ci, carry):
        r0 = pl.multiple_of(ci * L, L)
        rows = pl.ds(r0, L)
        prev_rows = pl.ds(pl.multiple_of(jnp.maximum(r0 - BF16_ROWS, 0), BF16_ROWS), BF16_ROWS)

        sm = sm_ref[rows, :]
        tc = sm + bifr_ref[...]
        gdec_c = -jnp.exp(arow_ref[...]) * _softplus(sm + drow_ref[...])
        gc = jnp.where(is_lf_c, _log_sigmoid(tc), jnp.where(is_da_c, gdec_c, 0.0)) * LOG2E
        bc = _sel_rows(tri_ref[...], gc, 3)
        tc = tc * LOG2E
        smt = smt_ref[ci]
        tr = smt + bifc_ref[...]
        gdec_r = -jnp.exp(acol_ref[...]) * _softplus(smt + dcol_ref[...])
        gr = jnp.where(is_lf_r, _log_sigmoid(tr), jnp.where(is_da_r, gdec_r, 0.0)) * LOG2E
        br = _sel_cols(gr, trit_ref[...], 3)
        tr = tr * LOG2E
        beta_all = pltpu.roll(_sigmoid(sm), SM_DA - SM_BETA, 1)
        eb_all = jnp.exp2(bc)
        beb_all = beta_all * eb_all
        kdec_all = jnp.exp2(bc[L - 1:L, :] - bc)
        gpre = _dot(sm, wlr_ref[...]) + bg_ref[...]
        glog_all = _log_sigmoid(gpre) * (LOG2E / GLA_TAU)

        def mlstm_head(h):
            cols = slice(h * HEAD_DIM, (h + 1) * HEAD_DIM)
            q = pm_ref[rows, h * HEAD_DIM:(h + 1) * HEAD_DIM]
            k = pm_ref[rows, W + h * HEAD_DIM:W + (h + 1) * HEAD_DIM]
            v = pm_ref[rows, 2 * W + h * HEAD_DIM:2 * W + (h + 1) * HEAD_DIM]
            b_c = bc[:, SM_LF + h:SM_LF + h + 1]
            li_c = tc[:, SM_LI + h:SM_LI + h + 1]
            b_r = br[SM_LF + h:SM_LF + h + 1, :]
            li_r = tr[SM_LI + h:SM_LI + h + 1, :]
            m_old = m_sc[h][0:1, 0:1]
            n_old = n_sc[h][0:1, :]
            c_old = c_sc[h]

            s_qk = _dot_nt(q, k)
            q_c = _dot(q, c_old)
            yield
            d = jnp.where(causal, b_c + (li_r - b_r), NEG_BIG)
            inter = b_c + m_old
            m_t = jnp.maximum(inter, jnp.max(d, axis=1, keepdims=True))
            p = jnp.exp2(d - m_t) * s_qk
            w_inter = jnp.exp2(inter - m_t)
            b_last = b_c[L - 1:L, :]
            g_s = b_last - b_c + li_c
            m_new = jnp.maximum(b_last + m_old, jnp.max(g_s, axis=0, keepdims=True))
            w_s = jnp.exp2(g_s - m_new)
            w_old = jnp.exp2(b_last + m_old - m_new)
            kw = k.astype(F32) * w_s
            pv = _dot(p, v)
            kv = _dot_tn(kw, v)
            yield
            num = pv + w_inter * q_c
            den = (jnp.sum(p, axis=1, keepdims=True)
                   + w_inter * jnp.sum(q.astype(F32) * n_old, axis=1, keepdims=True))
            hout = num / jnp.maximum(jnp.abs(den), jnp.exp2(-m_t))
            c_sc[h] = w_old * c_old + kv
            n_new = w_old * n_old + jnp.sum(kw, axis=0, keepdims=True)
            n_sc[h] = jnp.broadcast_to(n_new, (SUBLANES, LANES))
            m_sc[h] = jnp.broadcast_to(m_new, (SUBLANES, LANES))
            ogate = pm_ref[rows, 3 * W + h * HEAD_DIM:3 * W + (h + 1) * HEAD_DIM].astype(F32)
            gated = _sigmoid_of_twice(ogate) * hout
            out_ref[rows, cols] = _head_rmsnorm(gated, ghm_ref[:, cols]).astype(out_ref.dtype)

        def gla_head(h):
            cols = slice(h * HEAD_DIM, (h + 1) * HEAD_DIM)
            qb = pg_ref[rows, h * HEAD_DIM:(h + 1) * HEAD_DIM]
            kb = pg_ref[rows, W + h * HEAD_DIM:W + (h + 1) * HEAD_DIM]
            q = qb.astype(F32)
            k = kb.astype(F32)
            v = pg_ref[rows, 2 * W + h * HEAD_DIM:2 * W + (h + 1) * HEAD_DIM]
            e = _sel_rows(segm_ref[...], glog_all[:, cols], 2)
            yield
            bsum = e[nlow * L:(nlow + 1) * L]
            st_old = st_sc[h]
            q_st = _dot_nt(q * jnp.exp2(bsum), st_old)
            vk = _dot_tn(v, k * jnp.exp2(bsum[L - 1:L, :] - bsum))
            yield
            st_sc[h] = jnp.exp2(bsum[L - 1:L, :]) * st_old + vk
            a = None
            for lv in range(nlev):
                if lv < nlow:
                    lw = e[lv * L:(lv + 1) * L]
                else:
                    half = 1 << lv
                    b3 = bsum.reshape(L // (2 * half), 2 * half, HEAD_DIM)
                    lw = (b3 - b3[:, half - 1:half, :]).reshape(L, HEAD_DIM) * lvsign_ref[lv - nlow]
                ex = jnp.exp2(lw).astype(BF16)
                a_lv = lvmask_ref[lv] * _dot_nt(qb * ex, kb * ex)
                a = a_lv if a is None else a + a_lv
                yield
            dqk = jnp.sum(q * k, axis=1, keepdims=True)
            o = _dot(a, v) + dqk * v.astype(F32) + q_st
            yield
            rgate = pg_ref[rows, 3 * W + h * HEAD_DIM:3 * W + (h + 1) * HEAD_DIM].astype(F32)
            hn = _head_rmsnorm(o, ghl_ref[:, cols]) * _silu_of_twice(rgate)
            out_ref[rows, W + h * HEAD_DIM:W + (h + 1) * HEAD_DIM] = hn.astype(out_ref.dtype)

        def gdn_conv(piece, h, l2_mult):
            c0 = piece * W + h * HEAD_DIM
            xb = pd_ref[rows, c0:c0 + HEAD_DIM]
            x = xb.astype(F32)
            w = cw_ref[:, c0:c0 + HEAD_DIM]
            prev = jnp.where(ci == 0, tail_sc[:, c0:c0 + HEAD_DIM].astype(F32),
                             pd_ref[prev_rows, c0:c0 + HEAD_DIM].astype(F32))
            z = jnp.concatenate([prev[BF16_ROWS - SUBLANES:, :], x], axis=0)
            y = x * w[CONV_K - 1:CONV_K, :]
            for dlt in range(1, CONV_K):
                y = y + z[SUBLANES - dlt:SUBLANES - dlt + L, :] * w[CONV_K - 1 - dlt:CONV_K - dlt, :]
            y = _silu_of_twice(y)
            if l2_mult is not None:
                y = y * (lax.rsqrt(jnp.sum(y * y, axis=-1, keepdims=True) + EPS) * l2_mult)
            return y

        def gdn_head(h):
            cols = slice(h * HEAD_DIM, (h + 1) * HEAD_DIM)
            lane = slice(SM_DA + h, SM_DA + h + 1)
            q = gdn_conv(0, h, HEAD_DIM ** -0.5)
            k = gdn_conv(1, h, 1.0)
            v = gdn_conv(2, h, None)
            b_c = bc[:, lane]
            b_r = br[SM_DA + h:SM_DA + h + 1, :]
            beta = beta_all[:, lane]
            s_old = s_sc[h]

            kk = _dot_nt(k, k)
            yield
            decay = jnp.where(causal, jnp.exp2(jnp.minimum(b_c - b_r, 0.0)), 0.0)
            a = (beta * decay) * kk
            n = yield from _unit_lower_inverse_minus_eye(a, blk_masks)
            rhs = jnp.concatenate([beb_all[:, lane] * k, beta * v], axis=1)
            n_rhs = _dot(n, rhs)
            qk_raw = _dot_nt(q, k)
            yield
            wu = rhs + n_rhs
            w, u = wu[:, :HEAD_DIM], wu[:, HEAD_DIM:]
            ws_qs = _dot(jnp.concatenate([w, q * eb_all[:, lane]], axis=0), s_old)
            yield
            v_new = u - ws_qs[:L]
            qk_v = _dot(qk_raw * decay, v_new)
            k_v = _dot_tn(k * kdec_all[:, lane], v_new)
            yield
            o = ws_qs[L:] + qk_v
            s_sc[h] = eb_all[L - 1:L, lane] * s_old + k_v
            zgate = pd_ref[rows, 3 * W + h * HEAD_DIM:3 * W + (h + 1) * HEAD_DIM].astype(F32)
            hn = _head_rmsnorm(o, ghd_ref[:, cols]) * _silu_of_twice(zgate)
            out_ref[rows, 2 * W + h * HEAD_DIM:2 * W + (h + 1) * HEAD_DIM] = hn.astype(out_ref.dtype)

        gens = [f(h) for f in (gdn_head, gla_head, mlstm_head) for h in range(N_HEADS)]
        _round_robin(gens)
        return carry

    lax.fori_loop(0, nchunk, chunk, 0)
    tail_sc[...] = pd_ref[tb - BF16_ROWS:tb, 0:3 * W]


def _mixers(P, sm, smt, consts, B, S, tb):
    T = B * S
    nblk = S // tb
    row_blk = lambda b, c: b * nblk + c
    in_specs = [pl.BlockSpec((tb, GROUP_W), lambda b, c, g=g: (row_blk(b, c), g)) for g in range(3)]
    in_specs.append(pl.BlockSpec((tb, LANES), lambda b, c: (row_blk(b, c), 0)))
    in_specs.append(pl.BlockSpec((tb // CHUNK, SM_USED, CHUNK), lambda b, c: (row_blk(b, c), 0, 0)))
    in_specs += [_full_spec(a.shape) for a in consts]
    state = pltpu.VMEM((N_HEADS, HEAD_DIM, HEAD_DIM), F32)
    small = pltpu.VMEM((N_HEADS, SUBLANES, LANES), F32)
    return pl.pallas_call(
        _mixers_kernel,
        grid=(B, nblk),
        in_specs=in_specs,
        out_specs=pl.BlockSpec((tb, N_BRANCH * MIX_W), lambda b, c: (row_blk(b, c), 0)),
        out_shape=jax.ShapeDtypeStruct((T, N_BRANCH * MIX_W), BF16),
        scratch_shapes=[state, small, small, state, state,
                        pltpu.VMEM((BF16_ROWS, 3 * MIX_W), BF16)],
        compiler_params=pltpu.CompilerParams(
            dimension_semantics=("arbitrary", "arbitrary"), vmem_limit_bytes=VMEM_LIMIT),
        name="mixers",
    )(P, P, P, sm, smt, *consts)


def _merge_kernel(x_ref, h_ref, gates_ref, wup_ref, wout_ref, out_ref):
    d = x_ref.shape[1]
    acc = None
    for n in range(N_BRANCH):
        up = jnp.dot(h_ref[:, n * MIX_W:(n + 1) * MIX_W], wup_ref[n * MIX_W:(n + 1) * MIX_W, :],
                     preferred_element_type=F32)
        t = _sigmoid(gates_ref[:, n * d:(n + 1) * d].astype(F32)) * up
        acc = t if acc is None else acc + t
    out_ref[...] = x_ref[...] + jnp.dot(acc.astype(BF16), wout_ref[...], preferred_element_type=F32)


def _merge(x2, h, P, wup, wout, tm):
    T, D = x2.shape
    gate_blk = (3 * GROUP_W) // (N_BRANCH * D)
    row = lambda i: (i, 0)
    return pl.pallas_call(
        _merge_kernel,
        grid=(T // tm,),
        in_specs=[pl.BlockSpec((tm, D), row), pl.BlockSpec((tm, N_BRANCH * MIX_W), row),
                  pl.BlockSpec((tm, N_BRANCH * D), lambda i: (i, gate_blk)),
                  _full_spec((N_BRANCH * MIX_W, D)), _full_spec((D, D))],
        out_specs=pl.BlockSpec((tm, D), row),
        out_shape=jax.ShapeDtypeStruct((T, D), F32),
        compiler_params=pltpu.CompilerParams(
            dimension_semantics=("arbitrary",), vmem_limit_bytes=VMEM_LIMIT),
        name="merge",
    )(x2, h, P, wup, wout)


def _rmsnorm(x, g_row):
    return x * lax.rsqrt(jnp.mean(x * x, axis=-1, keepdims=True) + EPS) * g_row


def _mlp_kernel(x_ref, g_ref, w1_ref, w2_ref, gout_ref, out_ref, u_sc, acc_sc, *, norm_out):
    f = pl.program_id(1)

    @pl.when(f == 0)
    def _():
        u_sc[...] = _rmsnorm(x_ref[...], g_ref[...]).astype(BF16)
        acc_sc[...] = jnp.zeros_like(acc_sc)

    hmid = jnp.maximum(jnp.dot(u_sc[...], w1_ref[...], preferred_element_type=F32), 0.0)
    acc_sc[...] += jnp.dot((hmid * hmid).astype(BF16), w2_ref[...], preferred_element_type=F32)

    @pl.when(f == pl.num_programs(1) - 1)
    def _():
        y = x_ref[...] + acc_sc[...]
        out_ref[...] = _rmsnorm(y, gout_ref[...]) if norm_out else y


def _mlp(x2, g_row, w1, w2, tm, tf, g_out_row=None):
    T, D = x2.shape
    F = w1.shape[1]
    norm_out = g_out_row is not None
    return pl.pallas_call(
        functools.partial(_mlp_kernel, norm_out=norm_out),
        grid=(T // tm, F // tf),
        in_specs=[pl.BlockSpec((tm, D), lambda i, f: (i, 0)),
                  pl.BlockSpec((1, D), lambda i, f: (0, 0)),
                  pl.BlockSpec((D, tf), lambda i, f: (0, f)),
                  pl.BlockSpec((tf, D), lambda i, f: (f, 0)),
                  pl.BlockSpec((1, D), lambda i, f: (0, 0))],
        out_specs=pl.BlockSpec((tm, D), lambda i, f: (i, 0)),
        out_shape=jax.ShapeDtypeStruct((T, D), F32),
        scratch_shapes=[pltpu.VMEM((tm, D), BF16), pltpu.VMEM((tm, D), F32)],
        compiler_params=pltpu.CompilerParams(
            dimension_semantics=("arbitrary", "arbitrary"), vmem_limit_bytes=VMEM_LIMIT),
        name="mlp",
    )(x2, g_row, w1, w2, g_out_row if norm_out else g_row)


def _tile(n, want):
    t = min(n, want)
    while n % t:
        t //= 2
    return t


def _pad_rows(vals, offset, width=LANES):
    out = jnp.zeros((vals.shape[0], 1, width), F32)
    return out.at[:, 0, offset:offset + vals.shape[1]].set(vals.astype(F32))


def _as_cols(rows):
    return jnp.broadcast_to(rows[:, 0, :SM_USED, None], (rows.shape[0], SM_USED, CHUNK))


def kernel(x, w_in, b_if, w_gla_lr, b_gla, conv_gdn, a_log, dt_bias, g_norm_mix, g_norm_mlp,
           g_head_mlstm, g_head_gla, g_head_gdn, w_up, w_out, w_mlp_in, w_mlp_out, g_final):
    B, S, D = x.shape
    depth = w_in.shape[0]
    T = B * S
    H, W = N_HEADS, MIX_W
    assert S % CHUNK == 0 and D == D_MODEL

    tri_np, lvm_np, lvmask_np = _np_consts(CHUNK)
    tri = jnp.asarray(np.tile(tri_np, (1, 3)), BF16)
    trit = jnp.asarray(np.tile(tri_np.T, (3, 1)), BF16)
    n_low = int(np.log2(SUBLANES))
    segm = jnp.asarray(np.tile(np.concatenate(list(lvm_np[:n_low]) + [tri_np], axis=0), (1, 2)), BF16)
    lvmask = jnp.asarray(lvmask_np, F32)
    ti = np.arange(CHUNK)[:, None]
    ui = np.arange(CHUNK)[None, :]
    lvsign = jnp.asarray(np.stack([
        np.broadcast_to(np.where((ti // (1 << lv)) % 2 == 1, 1.0, -1.0), (CHUNK, HEAD_DIM))
        for lv in range(n_low, lvmask_np.shape[0])]), F32)
    blk_list = [((ti // SOLVE_BASE) == (ui // SOLVE_BASE)) & (ui < ti)]
    c = SOLVE_BASE
    while c < CHUNK:
        blk_list.append(((ti // (2 * c)) == (ui // (2 * c))) & ((ti // c) != (ui // c)))
        c *= 2
    blk = jnp.asarray(np.stack(blk_list), F32)

    sizes = [W, W, W, W, H, H, W, W, W, W, GLA_RANK, W, W, W, W, H, H, N_BRANCH * D]
    offs = [int(o) for o in np.concatenate([[0], np.cumsum(sizes)])]
    q_scale = HEAD_DIM ** -0.5
    big_ids = (0, 1, 2, 3, 6, 7, 8, 9, 11, 12, 13, 14, 17)
    pieces, dst = [], 0
    scales = {0: q_scale, 6: q_scale, 3: 0.5, 9: 0.5, 14: 0.5}
    for i in big_ids:
        pieces.append((offs[i], dst, sizes[i], scales.get(i, 1.0)))
        dst += sizes[i]
    small_pieces = ((offs[4], 2 * H), (offs[10], GLA_RANK), (offs[15], 2 * H))
    assert all(p[0] % SUBLANES == 0 for p in pieces) and all(p[0] % SUBLANES == 0 for p in small_pieces)
    w_big, w_small = _regroup(jnp.swapaxes(w_in, 1, 2), tuple(pieces), small_pieces, LANES)
    to_cast = (w_up.reshape(depth, N_BRANCH * W, D), w_out, w_mlp_in, w_mlp_out)

    conv_half = 0.5 * conv_gdn.astype(F32)
    bif_rows = _pad_rows(b_if, SM_LI)
    a_rows = _pad_rows(a_log, SM_DA)
    dt_rows = _pad_rows(dt_bias, SM_DA)
    bif_cols, a_cols, dt_cols = _as_cols(bif_rows), _as_cols(a_rows), _as_cols(dt_rows)
    wlr = jnp.zeros((depth, LANES, W), F32).at[:, SM_LLR:SM_LLR + GLA_RANK].set(w_gla_lr).astype(BF16)

    tm_proj = _tile(T, 1024)
    tn_proj = 3072
    tb = _tile(S, 1024)
    tm_merge = _tile(T, 1024)
    tm_mlp = _tile(T, 1024)
    tf_mlp = _tile(w_mlp_in.shape[2], 2048)

    x2 = x.reshape(T, D)
    for l in range(depth):
        P, sm, smt, w_up_b, w_out_b, w_mlp_in_b, w_mlp_out_b = _proj(
            x2, g_norm_mix[l].reshape(1, D), w_big, w_small, to_cast, l, tm_proj, tn_proj)
        consts = (bif_rows[l], bif_cols[l], g_head_mlstm[l].reshape(1, W),
                  wlr[l], b_gla[l].reshape(1, W), g_head_gla[l].reshape(1, W), segm, lvmask, lvsign,
                  conv_half[l], a_rows[l], a_cols[l], dt_rows[l], dt_cols[l],
                  g_head_gdn[l].reshape(1, W), blk, tri, trit)
        h = _mixers(P, sm, smt, consts, B, S, tb)
        x2 = _merge(x2, h, P, w_up_b, w_out_b, tm_merge)
        x2 = _mlp(x2, g_norm_mlp[l].reshape(1, D), w_mlp_in_b, w_mlp_out_b, tm_mlp, tf_mlp,
                  g_final.reshape(1, D) if l == depth - 1 else None)
    return x2.reshape(B, S, D)
```

```python
import functools

import numpy as np
import jax
import jax.numpy as jnp
from jax import lax
from jax.experimental import pallas as pl
from jax.experimental.pallas import tpu as pltpu

F32 = jnp.float32
BF16 = jnp.bfloat16

N_HEADS = 4
HEAD_DIM = 128
MIX_W = N_HEADS * HEAD_DIM
N_BRANCH = 3
GLA_RANK = 16
GLA_TAU = 16.0
CONV_K = 4
EPS = 1e-6
D_MODEL = 1024

LANES = 128
SUBLANES = 8
BF16_ROWS = 16
CHUNK = 128
SOLVE_BASE = 16
NEG_BIG = -1e30
LOG2E = 1.4426950408889634
VMEM_LIMIT = 56 * 1024 * 1024

SM_LI, SM_LF, SM_LLR, SM_BETA, SM_DA = 0, 4, 8, 24, 28
SM_USED = 32
GROUP_W = 4 * MIX_W
N_BIG = 3 * GROUP_W + N_BRANCH * D_MODEL


def _dot(a, b):
    return jnp.dot(a.astype(BF16), b.astype(BF16), preferred_element_type=F32)


def _dot_nt(a, b):
    return lax.dot_general(a.astype(BF16), b.astype(BF16), (((1,), (1,)), ((), ())),
                           preferred_element_type=F32)


def _dot_tn(a, b):
    return lax.dot_general(a.astype(BF16), b.astype(BF16), (((0,), (0,)), ((), ())),
                           preferred_element_type=F32)


def _split_bf16(x, n):
    parts = []
    r = x
    for i in range(n):
        p = r.astype(BF16)
        parts.append(p)
        if i + 1 < n:
            r = r - p.astype(F32)
    return parts


def _sel_rows(m01_rep, x, n):
    return jnp.dot(m01_rep, jnp.concatenate(_split_bf16(x, n), axis=0), preferred_element_type=F32)


def _sel_cols(x, m01_rep, n):
    return jnp.dot(jnp.concatenate(_split_bf16(x, n), axis=1), m01_rep, preferred_element_type=F32)


def _log_sigmoid(x):
    return jnp.minimum(x, 0.0) - jnp.log(1.0 + jnp.exp(-jnp.abs(x)))


def _softplus(x):
    return jnp.maximum(x, 0.0) + jnp.log(1.0 + jnp.exp(-jnp.abs(x)))


def _sigmoid_of_twice(h):
    return 0.5 * jnp.tanh(h) + 0.5


def _silu_of_twice(h):
    return h + h * jnp.tanh(h)


def _sigmoid(x):
    return _sigmoid_of_twice(0.5 * x)


def _head_rmsnorm(h, g_row):
    return h * lax.rsqrt(jnp.mean(h * h, axis=-1, keepdims=True) + EPS) * g_row


def _round_robin(gens):
    live = list(gens)
    while live:
        nxt = []
        for g in live:
            try:
                next(g)
                nxt.append(g)
            except StopIteration:
                pass
        live = nxt


def _np_consts(L):
    t = np.arange(L)[:, None]
    u = np.arange(L)[None, :]
    tri = (u <= t)
    lv_m, lv_mask = [], []
    c = 1
    while c < L:
        base = (t // (2 * c)) * (2 * c)
        ref = base + c - 1
        second = t >= base + c
        m = np.where(second, (u > ref) & (u <= t), (u > t) & (u <= ref))
        ub = (u // (2 * c)) * (2 * c)
        mask = (base == ub) & second & (u < ub + c)
        lv_m.append(m)
        lv_mask.append(mask)
        c *= 2
    return tri, np.stack(lv_m), np.stack(lv_mask)


def _full_spec(shape):
    nd = len(shape)
    return pl.BlockSpec(shape, lambda *_: (0,) * nd)


def _regroup_kernel(wt_ref, big_ref, small_ref, *, big_pieces, small_pieces):
    for src, dst, n, scale in big_pieces:
        v = wt_ref[src:src + n, :]
        if scale != 1.0:
            v = v * scale
        big_ref[dst:dst + n, :] = v.astype(big_ref.dtype)
    small = [wt_ref[src:src + n, :] for src, n in small_pieces]
    n_small = sum(n for _, n in small_pieces)
    small.append(jnp.zeros((small_ref.shape[0] - n_small, wt_ref.shape[1]), F32))
    small_ref[...] = jnp.concatenate(small, axis=0).astype(small_ref.dtype)


def _regroup(w_in_t, big_pieces, small_pieces, tc):
    depth, n_in, D = w_in_t.shape
    return pl.pallas_call(
        functools.partial(_regroup_kernel, big_pieces=big_pieces, small_pieces=small_pieces),
        grid=(depth, D // tc),
        in_specs=[pl.BlockSpec((None, n_in, tc), lambda l, c: (l, 0, c))],
        out_specs=[pl.BlockSpec((None, N_BIG, tc), lambda l, c: (l, 0, c)),
                   pl.BlockSpec((None, LANES, tc), lambda l, c: (l, 0, c))],
        out_shape=[jax.ShapeDtypeStruct((depth, N_BIG, D), BF16),
                   jax.ShapeDtypeStruct((depth, LANES, D), BF16)],
        compiler_params=pltpu.CompilerParams(
            dimension_semantics=("arbitrary", "arbitrary"), vmem_limit_bytes=VMEM_LIMIT),
        name="regroup",
    )(w_in_t)


_NT_DIMS = (((1,), (1,)), ((), ()))


def _proj_kernel(x_ref, g_ref, wb_ref, ws_ref, *rest, n_cast):
    cast_in = rest[:n_cast]
    p_ref, sm_ref, smt_ref = rest[n_cast:n_cast + 3]
    cast_out = rest[n_cast + 3:2 * n_cast + 3]
    u_sc = rest[2 * n_cast + 3]
    j = pl.program_id(1)

    @pl.when(j == 0)
    def _():
        x = x_ref[...]
        u = x * lax.rsqrt(jnp.mean(x * x, axis=-1, keepdims=True) + EPS) * g_ref[...]
        ub = u.astype(BF16)
        u_sc[...] = ub
        sm_ref[...] = lax.dot_general(ub, ws_ref[...], _NT_DIMS, preferred_element_type=F32)
        smt = lax.dot_general(ws_ref[:SM_USED, :], ub, _NT_DIMS, preferred_element_type=F32)
        for c in range(smt_ref.shape[0]):
            smt_ref[c] = smt[:, c * CHUNK:(c + 1) * CHUNK]
        for src, dst in zip(cast_in, cast_out):
            dst[...] = src[...].astype(dst.dtype)

    p = lax.dot_general(u_sc[...], wb_ref[...], _NT_DIMS, preferred_element_type=F32)
    p_ref[...] = p.astype(p_ref.dtype)


def _proj(x2, g_row, w_big_t, w_small_t, to_cast, layer, tm, tn):
    T, D = x2.shape
    n_row = T // tm
    cast_in_specs, cast_out_specs, cast_out_shapes = [], [], []
    for w in to_cast:
        _, R, C = w.shape
        assert R % (n_row * BF16_ROWS) == 0
        cast_in_specs.append(pl.BlockSpec((None, R // n_row, C), lambda i, j: (layer, i, 0)))
        cast_out_specs.append(pl.BlockSpec((R // n_row, C), lambda i, j: (i, 0)))
        cast_out_shapes.append(jax.ShapeDtypeStruct((R, C), BF16))
    return pl.pallas_call(
        functools.partial(_proj_kernel, n_cast=len(to_cast)),
        grid=(n_row, N_BIG // tn),
        in_specs=[
            pl.BlockSpec((tm, D), lambda i, j: (i, 0)),
            pl.BlockSpec((1, D), lambda i, j: (0, 0)),
            pl.BlockSpec((None, tn, D), lambda i, j: (layer, j, 0)),
            pl.BlockSpec((None, LANES, D), lambda i, j: (layer, 0, 0)),
        ] + cast_in_specs,
        out_specs=[
            pl.BlockSpec((tm, tn), lambda i, j: (i, j)),
            pl.BlockSpec((tm, LANES), lambda i, j: (i, 0)),
            pl.BlockSpec((tm // CHUNK, SM_USED, CHUNK), lambda i, j: (i, 0, 0)),
        ] + cast_out_specs,
        out_shape=[
            jax.ShapeDtypeStruct((T, N_BIG), BF16),
            jax.ShapeDtypeStruct((T, LANES), F32),
            jax.ShapeDtypeStruct((T // CHUNK, SM_USED, CHUNK), F32),
        ] + cast_out_shapes,
        scratch_shapes=[pltpu.VMEM((tm, D), BF16)],
        compiler_params=pltpu.CompilerParams(
            dimension_semantics=("arbitrary", "arbitrary"), vmem_limit_bytes=VMEM_LIMIT),
        name="proj",
    )(x2, g_row, w_big_t, w_small_t, *to_cast)


def _unit_lower_inverse_minus_eye(a, blk_masks):
    a0 = a * blk_masks[0]
    n = -a0
    pw = _dot(a0, a0)
    yield
    span = 2
    while span < SOLVE_BASE:
        n_pw = _dot(n, pw)
        pw_next = _dot(pw, pw) if 2 * span < SOLVE_BASE else None
        yield
        n = n + pw + n_pw
        pw = pw_next
        span *= 2
    for m in blk_masks[1:]:
        c = a * m
        n_c = _dot(n, c)
        yield
        t = c + n_c
        t_n = _dot(t, n)
        yield
        n = n - t - t_n
    return n


def _mixers_kernel(pm_ref, pg_ref, pd_ref, sm_ref, smt_ref,
                   bifr_ref, bifc_ref, ghm_ref,
                   wlr_ref, bg_ref, ghl_ref, segm_ref, lvmask_ref, lvsign_ref,
                   cw_ref, arow_ref, acol_ref, drow_ref, dcol_ref, ghd_ref, blk_ref,
                   tri_ref, trit_ref,
                   out_ref,
                   c_sc, n_sc, m_sc, st_sc, s_sc, tail_sc, zbuf_sc):
    L = CHUNK
    W = MIX_W
    tb = pm_ref.shape[0]
    nchunk = tb // L
    nlev = lvmask_ref.shape[0]
    nlow = segm_ref.shape[0] // L - 1
    blk_masks = [blk_ref[i] for i in range(blk_ref.shape[0])]

    @pl.when(pl.program_id(1) == 0)
    def _():
        c_sc[...] = jnp.zeros_like(c_sc)
        n_sc[...] = jnp.zeros_like(n_sc)
        m_sc[...] = jnp.zeros_like(m_sc)
        st_sc[...] = jnp.zeros_like(st_sc)
        s_sc[...] = jnp.zeros_like(s_sc)
        tail_sc[...] = jnp.zeros_like(tail_sc)

    row_i = lax.broadcasted_iota(jnp.int32, (L, L), 0)
    col_i = lax.broadcasted_iota(jnp.int32, (L, L), 1)
    causal = col_i <= row_i
    sm_col = lax.broadcasted_iota(jnp.int32, (L, LANES), 1)
    is_lf_c = (sm_col >= SM_LF) & (sm_col < SM_LF + N_HEADS)
    is_da_c = (sm_col >= SM_DA) & (sm_col < SM_DA + N_HEADS)
    smt_row = lax.broadcasted_iota(jnp.int32, (SM_USED, L), 0)
    is_lf_r = (smt_row >= SM_LF) & (smt_row < SM_LF + N_HEADS)
    is_da_r = (smt_row >= SM_DA) & (smt_row < SM_DA + N_HEADS)

    def chunk(ci, carry):
        r0 = pl.multiple_of(ci * L, L)
        rows = pl.ds(r0, L)
        prev_rows = pl.ds(pl.multiple_of(jnp.maximum(r0 - BF16_ROWS, 0), BF16_ROWS), BF16_ROWS)

        sm = sm_ref[rows, :]
        tc = sm + bifr_ref[...]
        gdec_c = -jnp.exp(arow_ref[...]) * _softplus(sm + drow_ref[...])
        gc = jnp.where(is_lf_c, _log_sigmoid(tc), jnp.where(is_da_c, gdec_c, 0.0)) * LOG2E
        bc = _sel_rows(tri_ref[...], gc, 3)
        tc = tc * LOG2E
        smt = smt_ref[ci]
        tr = smt + bifc_ref[...]
        gdec_r = -jnp.exp(acol_ref[...]) * _softplus(smt + dcol_ref[...])
        gr = jnp.where(is_lf_r, _log_sigmoid(tr), jnp.where(is_da_r, gdec_r, 0.0)) * LOG2E
        br = _sel_cols(gr, trit_ref[...], 3)
        tr = tr * LOG2E
        beta_all = pltpu.roll(_sigmoid(sm), SM_DA - SM_BETA, 1)
        eb_all = jnp.exp2(bc)
        beb_all = beta_all * eb_all
        kdec_all = jnp.exp2(bc[L - 1:L, :] - bc)
        gpre = _dot(sm, wlr_ref[...]) + bg_ref[...]
        glog_all = _log_sigmoid(gpre) * (LOG2E / GLA_TAU)

        def mlstm_head(h):
            cols = slice(h * HEAD_DIM, (h + 1) * HEAD_DIM)
            q = pm_ref[rows, h * HEAD_DIM:(h + 1) * HEAD_DIM]
            k = pm_ref[rows, W + h * HEAD_DIM:W + (h + 1) * HEAD_DIM]
            v = pm_ref[rows, 2 * W + h * HEAD_DIM:2 * W + (h + 1) * HEAD_DIM]
            b_c = bc[:, SM_LF + h:SM_LF + h + 1]
            li_c = tc[:, SM_LI + h:SM_LI + h + 1]
            b_r = br[SM_LF + h:SM_LF + h + 1, :]
            li_r = tr[SM_LI + h:SM_LI + h + 1, :]
            m_old = m_sc[h][0:1, 0:1]
            n_old = n_sc[h][0:1, :]
            c_old = c_sc[h]

            s_qk = _dot_nt(q, k)
            q_c = _dot(q, c_old)
            yield
            d = jnp.where(causal, b_c + (li_r - b_r), NEG_BIG)
            inter = b_c + m_old
            m_t = jnp.maximum(inter, jnp.max(d, axis=1, keepdims=True))
            p = jnp.exp2(d - m_t) * s_qk
            w_inter = jnp.exp2(inter - m_t)
            b_last = b_c[L - 1:L, :]
            g_s = b_last - b_c + li_c
            m_new = jnp.maximum(b_last + m_old, jnp.max(g_s, axis=0, keepdims=True))
            w_s = jnp.exp2(g_s - m_new)
            w_old = jnp.exp2(b_last + m_old - m_new)
            kw = k.astype(F32) * w_s
            pv = _dot(p, v)
            kv = _dot_tn(kw, v)
            yield
            num = pv + w_inter * q_c
            den = (jnp.sum(p, axis=1, keepdims=True)
                   + w_inter * jnp.sum(q.astype(F32) * n_old, axis=1, keepdims=True))
            hout = num / jnp.maximum(jnp.abs(den), jnp.exp2(-m_t))
            c_sc[h] = w_old * c_old + kv
            n_new = w_old * n_old + jnp.sum(kw, axis=0, keepdims=True)
            n_sc[h] = jnp.broadcast_to(n_new, (SUBLANES, LANES))
            m_sc[h] = jnp.broadcast_to(m_new, (SUBLANES, LANES))
            ogate = pm_ref[rows, 3 * W + h * HEAD_DIM:3 * W + (h + 1) * HEAD_DIM].astype(F32)
            gated = _sigmoid_of_twice(ogate) * hout
            out_ref[rows, cols] = _head_rmsnorm(gated, ghm_ref[:, cols]).astype(out_ref.dtype)

        def gla_head(h):
            cols = slice(h * HEAD_DIM, (h + 1) * HEAD_DIM)
            qb = pg_ref[rows, h * HEAD_DIM:(h + 1) * HEAD_DIM]
            kb = pg_ref[rows, W + h * HEAD_DIM:W + (h + 1) * HEAD_DIM]
            q = qb.astype(F32)
            k = kb.astype(F32)
            v = pg_ref[rows, 2 * W + h * HEAD_DIM:2 * W + (h + 1) * HEAD_DIM]
            e = _sel_rows(segm_ref[...], glog_all[:, cols], 2)
            yield
            bsum = e[nlow * L:(nlow + 1) * L]
            st_old = st_sc[h]
            q_st = _dot_nt(q * jnp.exp2(bsum), st_old)
            vk = _dot_tn(v, k * jnp.exp2(bsum[L - 1:L, :] - bsum))
            yield
            st_sc[h] = jnp.exp2(bsum[L - 1:L, :]) * st_old + vk
            a = None
            for lv in range(nlev):
                if lv < nlow:
                    lw = e[lv * L:(lv + 1) * L]
                else:
                    half = 1 << lv
                    b3 = bsum.reshape(L // (2 * half), 2 * half, HEAD_DIM)
                    lw = (b3 - b3[:, half - 1:half, :]).reshape(L, HEAD_DIM) * lvsign_ref[lv - nlow]
                ex = jnp.exp2(lw).astype(BF16)
                a_lv = lvmask_ref[lv] * _dot_nt(qb * ex, kb * ex)
                a = a_lv if a is None else a + a_lv
                yield
            dqk = jnp.sum(q * k, axis=1, keepdims=True)
            o = _dot(a, v) + dqk * v.astype(F32) + q_st
            yield
            rgate = pg_ref[rows, 3 * W + h * HEAD_DIM:3 * W + (h + 1) * HEAD_DIM].astype(F32)
            hn = _head_rmsnorm(o, ghl_ref[:, cols]) * _silu_of_twice(rgate)
            out_ref[rows, W + h * HEAD_DIM:W + (h + 1) * HEAD_DIM] = hn.astype(out_ref.dtype)

        def gdn_conv(piece, h, l2_mult):
            c0 = piece * W + h * HEAD_DIM
            xb = pd_ref[rows, c0:c0 + HEAD_DIM]
            x = xb.astype(F32)
            w = cw_ref[:, c0:c0 + HEAD_DIM]
            prev = jnp.where(ci == 0, tail_sc[:, c0:c0 + HEAD_DIM].astype(F32),
                             pd_ref[prev_rows, c0:c0 + HEAD_DIM].astype(F32))
            zb = zbuf_sc.at[piece * N_HEADS + h]
            zb[0:SUBLANES, :] = prev[BF16_ROWS - SUBLANES:, :]
            zb[SUBLANES:SUBLANES + L, :] = x
            y = x * w[CONV_K - 1:CONV_K, :]
            for dlt in range(1, CONV_K):
                y = y + zb[SUBLANES - dlt:SUBLANES - dlt + L, :] * w[CONV_K - 1 - dlt:CONV_K - dlt, :]
            y = _silu_of_twice(y)
            if l2_mult is not None:
                y = y * (lax.rsqrt(jnp.sum(y * y, axis=-1, keepdims=True) + EPS) * l2_mult)
            return y

        def gdn_head(h):
            cols = slice(h * HEAD_DIM, (h + 1) * HEAD_DIM)
            lane = slice(SM_DA + h, SM_DA + h + 1)
            q = gdn_conv(0, h, HEAD_DIM ** -0.5)
            k = gdn_conv(1, h, 1.0)
            v = gdn_conv(2, h, None)
            b_c = bc[:, lane]
            b_r = br[SM_DA + h:SM_DA + h + 1, :]
            beta = beta_all[:, lane]
            s_old = s_sc[h]

            kk = _dot_nt(k, k)
            yield
            decay = jnp.where(causal, jnp.exp2(jnp.minimum(b_c - b_r, 0.0)), 0.0)
            a = (beta * decay) * kk
            n = yield from _unit_lower_inverse_minus_eye(a, blk_masks)
            rhs = jnp.concatenate([beb_all[:, lane] * k, beta * v], axis=1)
            n_rhs = _dot(n, rhs)
            qk_raw = _dot_nt(q, k)
            yield
            wu = rhs + n_rhs
            w, u = wu[:, :HEAD_DIM], wu[:, HEAD_DIM:]
            ws_qs = _dot(jnp.concatenate([w, q * eb_all[:, lane]], axis=0), s_old)
            yield
            v_new = u - ws_qs[:L]
            qk_v = _dot(qk_raw * decay, v_new)
            k_v = _dot_tn(k * kdec_all[:, lane], v_new)
            yield
            o = ws_qs[L:] + qk_v
            s_sc[h] = eb_all[L - 1:L, lane] * s_old + k_v
            zgate = pd_ref[rows, 3 * W + h * HEAD_DIM:3 * W + (h + 1) * HEAD_DIM].astype(F32)
            hn = _head_rmsnorm(o, ghd_ref[:, cols]) * _silu_of_twice(zgate)
            out_ref[rows, 2 * W + h * HEAD_DIM:2 * W + (h + 1) * HEAD_DIM] = hn.astype(out_ref.dtype)

        gens = [f(h) for f in (gdn_head, gla_head, mlstm_head) for h in range(N_HEADS)]
        _round_robin(gens)
        return carry

    lax.fori_loop(0, nchunk, chunk, 0)
    tail_sc[...] = pd_ref[tb - BF16_ROWS:tb, 0:3 * W]


def _mixers(P, sm, smt, consts, B, S, tb):
    T = B * S
    nblk = S // tb
    row_blk = lambda b, c: b * nblk + c
    in_specs = [pl.BlockSpec((tb, GROUP_W), lambda b, c, g=g: (row_blk(b, c), g)) for g in range(3)]
    in_specs.append(pl.BlockSpec((tb, LANES), lambda b, c: (row_blk(b, c), 0)))
    in_specs.append(pl.BlockSpec((tb // CHUNK, SM_USED, CHUNK), lambda b, c: (row_blk(b, c), 0, 0)))
    in_specs += [_full_spec(a.shape) for a in consts]
    state = pltpu.VMEM((N_HEADS, HEAD_DIM, HEAD_DIM), F32)
    small = pltpu.VMEM((N_HEADS, SUBLANES, LANES), F32)
    return pl.pallas_call(
        _mixers_kernel,
        grid=(B, nblk),
        in_specs=in_specs,
        out_specs=pl.BlockSpec((tb, N_BRANCH * MIX_W), lambda b, c: (row_blk(b, c), 0)),
        out_shape=jax.ShapeDtypeStruct((T, N_BRANCH * MIX_W), BF16),
        scratch_shapes=[state, small, small, state, state,
                        pltpu.VMEM((BF16_ROWS, 3 * MIX_W), BF16),
                        pltpu.VMEM((3 * N_HEADS, SUBLANES + CHUNK, HEAD_DIM), F32)],
        compiler_params=pltpu.CompilerParams(
            dimension_semantics=("arbitrary", "arbitrary"), vmem_limit_bytes=VMEM_LIMIT),
        name="mixers",
    )(P, P, P, sm, smt, *consts)


def _merge_kernel(x_ref, h_ref, gates_ref, wup_ref, wout_ref, out_ref):
    d = x_ref.shape[1]
    acc = None
    for n in range(N_BRANCH):
        up = jnp.dot(h_ref[:, n * MIX_W:(n + 1) * MIX_W], wup_ref[n * MIX_W:(n + 1) * MIX_W, :],
                     preferred_element_type=F32)
        t = _sigmoid(gates_ref[:, n * d:(n + 1) * d].astype(F32)) * up
        acc = t if acc is None else acc + t
    out_ref[...] = x_ref[...] + jnp.dot(acc.astype(BF16), wout_ref[...], preferred_element_type=F32)


def _merge(x2, h, P, wup, wout, tm):
    T, D = x2.shape
    gate_blk = (3 * GROUP_W) // (N_BRANCH * D)
    row = lambda i: (i, 0)
    return pl.pallas_call(
        _merge_kernel,
        grid=(T // tm,),
        in_specs=[pl.BlockSpec((tm, D), row), pl.BlockSpec((tm, N_BRANCH * MIX_W), row),
                  pl.BlockSpec((tm, N_BRANCH * D), lambda i: (i, gate_blk)),
                  _full_spec((N_BRANCH * MIX_W, D)), _full_spec((D, D))],
        out_specs=pl.BlockSpec((tm, D), row),
        out_shape=jax.ShapeDtypeStruct((T, D), F32),
        compiler_params=pltpu.CompilerParams(
            dimension_semantics=("arbitrary",), vmem_limit_bytes=VMEM_LIMIT),
        name="merge",
    )(x2, h, P, wup, wout)


def _rmsnorm(x, g_row):
    return x * lax.rsqrt(jnp.mean(x * x, axis=-1, keepdims=True) + EPS) * g_row


def _mlp_kernel(x_ref, g_ref, w1_ref, w2_ref, gout_ref, out_ref, u_sc, acc_sc, *, norm_out):
    f = pl.program_id(1)

    @pl.when(f == 0)
    def _():
        u_sc[...] = _rmsnorm(x_ref[...], g_ref[...]).astype(BF16)
        acc_sc[...] = jnp.zeros_like(acc_sc)

    hmid = jnp.maximum(jnp.dot(u_sc[...], w1_ref[...], preferred_element_type=F32), 0.0)
    acc_sc[...] += jnp.dot((hmid * hmid).astype(BF16), w2_ref[...], preferred_element_type=F32)

    @pl.when(f == pl.num_programs(1) - 1)
    def _():
        y = x_ref[...] + acc_sc[...]
        out_ref[...] = _rmsnorm(y, gout_ref[...]) if norm_out else y


def _mlp(x2, g_row, w1, w2, tm, tf, g_out_row=None):
    T, D = x2.shape
    F = w1.shape[1]
    norm_out = g_out_row is not None
    return pl.pallas_call(
        functools.partial(_mlp_kernel, norm_out=norm_out),
        grid=(T // tm, F // tf),
        in_specs=[pl.BlockSpec((tm, D), lambda i, f: (i, 0)),
                  pl.BlockSpec((1, D), lambda i, f: (0, 0)),
                  pl.BlockSpec((D, tf), lambda i, f: (0, f)),
                  pl.BlockSpec((tf, D), lambda i, f: (f, 0)),
                  pl.BlockSpec((1, D), lambda i, f: (0, 0))],
        out_specs=pl.BlockSpec((tm, D), lambda i, f: (i, 0)),
        out_shape=jax.ShapeDtypeStruct((T, D), F32),
        scratch_shapes=[pltpu.VMEM((tm, D), BF16), pltpu.VMEM((tm, D), F32)],
        compiler_params=pltpu.CompilerParams(
            dimension_semantics=("arbitrary", "arbitrary"), vmem_limit_bytes=VMEM_LIMIT),
        name="mlp",
    )(x2, g_row, w1, w2, g_out_row if norm_out else g_row)


def _tile(n, want):
    t = min(n, want)
    while n % t:
        t //= 2
    return t


def _pad_rows(vals, offset, width=LANES):
    out = jnp.zeros((vals.shape[0], 1, width), F32)
    return out.at[:, 0, offset:offset + vals.shape[1]].set(vals.astype(F32))


def _as_cols(rows):
    return jnp.broadcast_to(rows[:, 0, :SM_USED, None], (rows.shape[0], SM_USED, CHUNK))


def kernel(x, w_in, b_if, w_gla_lr, b_gla, conv_gdn, a_log, dt_bias, g_norm_mix, g_norm_mlp,
           g_head_mlstm, g_head_gla, g_head_gdn, w_up, w_out, w_mlp_in, w_mlp_out, g_final):
    B, S, D = x.shape
    depth = w_in.shape[0]
    T = B * S
    H, W = N_HEADS, MIX_W
    assert S % CHUNK == 0 and D == D_MODEL

    tri_np, lvm_np, lvmask_np = _np_consts(CHUNK)
    tri = jnp.asarray(np.tile(tri_np, (1, 3)), BF16)
    trit = jnp.asarray(np.tile(tri_np.T, (3, 1)), BF16)
    n_low = int(np.log2(SUBLANES))
    segm = jnp.asarray(np.tile(np.concatenate(list(lvm_np[:n_low]) + [tri_np], axis=0), (1, 2)), BF16)
    lvmask = jnp.asarray(lvmask_np, F32)
    ti = np.arange(CHUNK)[:, None]
    ui = np.arange(CHUNK)[None, :]
    lvsign = jnp.asarray(np.stack([
        np.broadcast_to(np.where((ti // (1 << lv)) % 2 == 1, 1.0, -1.0), (CHUNK, HEAD_DIM))
        for lv in range(n_low, lvmask_np.shape[0])]), F32)
    blk_list = [((ti // SOLVE_BASE) == (ui // SOLVE_BASE)) & (ui < ti)]
    c = SOLVE_BASE
    while c < CHUNK:
        blk_list.append(((ti // (2 * c)) == (ui // (2 * c))) & ((ti // c) != (ui // c)))
        c *= 2
    blk = jnp.asarray(np.stack(blk_list), F32)

    sizes = [W, W, W, W, H, H, W, W, W, W, GLA_RANK, W, W, W, W, H, H, N_BRANCH * D]
    offs = [int(o) for o in np.concatenate([[0], np.cumsum(sizes)])]
    q_scale = HEAD_DIM ** -0.5
    big_ids = (0, 1, 2, 3, 6, 7, 8, 9, 11, 12, 13, 14, 17)
    pieces, dst = [], 0
    scales = {0: q_scale, 6: q_scale, 3: 0.5, 9: 0.5, 14: 0.5}
    for i in big_ids:
        pieces.append((offs[i], dst, sizes[i], scales.get(i, 1.0)))
        dst += sizes[i]
    small_pieces = ((offs[4], 2 * H), (offs[10], GLA_RANK), (offs[15], 2 * H))
    assert all(p[0] % SUBLANES == 0 for p in pieces) and all(p[0] % SUBLANES == 0 for p in small_pieces)
    w_big, w_small = _regroup(jnp.swapaxes(w_in, 1, 2), tuple(pieces), small_pieces, LANES)
    to_cast = (w_up.reshape(depth, N_BRANCH * W, D), w_out, w_mlp_in, w_mlp_out)

    conv_half = 0.5 * conv_gdn.astype(F32)
    bif_rows = _pad_rows(b_if, SM_LI)
    a_rows = _pad_rows(a_log, SM_DA)
    dt_rows = _pad_rows(dt_bias, SM_DA)
    bif_cols, a_cols, dt_cols = _as_cols(bif_rows), _as_cols(a_rows), _as_cols(dt_rows)
    wlr = jnp.zeros((depth, LANES, W), F32).at[:, SM_LLR:SM_LLR + GLA_RANK].set(w_gla_lr).astype(BF16)

    tm_proj = _tile(T, 1024)
    tn_proj = 3072
    tb = _tile(S, 1024)
    tm_merge = _tile(T, 1024)
    tm_mlp = _tile(T, 1024)
    tf_mlp = _tile(w_mlp_in.shape[2], 2048)

    x2 = x.reshape(T, D)
    for l in range(depth):
        P, sm, smt, w_up_b, w_out_b, w_mlp_in_b, w_mlp_out_b = _proj(
            x2, g_norm_mix[l].reshape(1, D), w_big, w_small, to_cast, l, tm_proj, tn_proj)
        consts = (bif_rows[l], bif_cols[l], g_head_mlstm[l].reshape(1, W),
                  wlr[l], b_gla[l].reshape(1, W), g_head_gla[l].reshape(1, W), segm, lvmask, lvsign,
                  conv_half[l], a_rows[l], a_cols[l], dt_rows[l], dt_cols[l],
                  g_head_gdn[l].reshape(1, W), blk, tri, trit)
        h = _mixers(P, sm, smt, consts, B, S, tb)
        x2 = _merge(x2, h, P, w_up_b, w_out_b, tm_merge)
        x2 = _mlp(x2, g_norm_mlp[l].reshape(1, D), w_mlp_in_b, w_mlp_out_b, tm_mlp, tf_mlp,
                  g_final.reshape(1, D) if l == depth - 1 else None)
    return x2.reshape(B, S, D)
```

```python
import functools

import numpy as np
import jax
import jax.numpy as jnp
from jax import lax
from jax.experimental import pallas as pl
from jax.experimental.pallas import tpu as pltpu

F32 = jnp.float32
BF16 = jnp.bfloat16

N_HEADS = 4
HEAD_DIM = 128
MIX_W = N_HEADS * HEAD_DIM
N_BRANCH = 3
GLA_RANK = 16
GLA_TAU = 16.0
CONV_K = 4
EPS = 1e-6
D_MODEL = 1024

LANES = 128
SUBLANES = 8
BF16_ROWS = 16
CHUNK = 128
SOLVE_BASE = 16
NEG_BIG = -1e30
LOG2E = 1.4426950408889634
VMEM_LIMIT = 56 * 1024 * 1024

SM_LI, SM_LF, SM_LLR, SM_BETA, SM_DA = 0, 4, 8, 24, 28
SM_USED = 32
GROUP_W = 4 * MIX_W
N_BIG = 3 * GROUP_W + N_BRANCH * D_MODEL


def _dot(a, b):
    return jnp.dot(a.astype(BF16), b.astype(BF16), preferred_element_type=F32)


def _dot_nt(a, b):
    return lax.dot_general(a.astype(BF16), b.astype(BF16), (((1,), (1,)), ((), ())),
                           preferred_element_type=F32)


def _dot_tn(a, b):
    return lax.dot_general(a.astype(BF16), b.astype(BF16), (((0,), (0,)), ((), ())),
                           preferred_element_type=F32)


def _split_bf16(x, n):
    parts = []
    r = x
    for i in range(n):
        p = r.astype(BF16)
        parts.append(p)
        if i + 1 < n:
            r = r - p.astype(F32)
    return parts


def _sel_rows(m01_rep, x, n):
    return jnp.dot(m01_rep, jnp.concatenate(_split_bf16(x, n), axis=0), preferred_element_type=F32)


def _sel_cols(x, m01_rep, n):
    return jnp.dot(jnp.concatenate(_split_bf16(x, n), axis=1), m01_rep, preferred_element_type=F32)


def _log_sigmoid(x):
    return jnp.minimum(x, 0.0) - jnp.log(1.0 + jnp.exp(-jnp.abs(x)))


def _softplus(x):
    return jnp.maximum(x, 0.0) + jnp.log(1.0 + jnp.exp(-jnp.abs(x)))


def _sigmoid_of_twice(h):
    return 0.5 * jnp.tanh(h) + 0.5


def _silu_of_twice(h):
    return h + h * jnp.tanh(h)


def _sigmoid(x):
    return _sigmoid_of_twice(0.5 * x)


def _head_rmsnorm(h, g_row):
    return h * lax.rsqrt(jnp.mean(h * h, axis=-1, keepdims=True) + EPS) * g_row


def _round_robin(gens):
    live = list(gens)
    while live:
        nxt = []
        for g in live:
            try:
                next(g)
                nxt.append(g)
            except StopIteration:
                pass
        live = nxt


def _np_consts(L):
    t = np.arange(L)[:, None]
    u = np.arange(L)[None, :]
    tri = (u <= t)
    lv_m, lv_mask = [], []
    c = 1
    while c < L:
        base = (t // (2 * c)) * (2 * c)
        ref = base + c - 1
        second = t >= base + c
        m = np.where(second, (u > ref) & (u <= t), (u > t) & (u <= ref))
        ub = (u // (2 * c)) * (2 * c)
        mask = (base == ub) & second & (u < ub + c)
        lv_m.append(m)
        lv_mask.append(mask)
        c *= 2
    return tri, np.stack(lv_m), np.stack(lv_mask)


def _full_spec(shape):
    nd = len(shape)
    return pl.BlockSpec(shape, lambda *_: (0,) * nd)


def _regroup_kernel(wt_ref, big_ref, small_ref, *, big_pieces, small_pieces):
    for src, dst, n, scale in big_pieces:
        v = wt_ref[src:src + n, :]
        if scale != 1.0:
            v = v * scale
        big_ref[dst:dst + n, :] = v.astype(big_ref.dtype)
    small = [wt_ref[src:src + n, :] for src, n in small_pieces]
    n_small = sum(n for _, n in small_pieces)
    small.append(jnp.zeros((small_ref.shape[0] - n_small, wt_ref.shape[1]), F32))
    small_ref[...] = jnp.concatenate(small, axis=0).astype(small_ref.dtype)


def _regroup(w_in_t, big_pieces, small_pieces, tc):
    depth, n_in, D = w_in_t.shape
    return pl.pallas_call(
        functools.partial(_regroup_kernel, big_pieces=big_pieces, small_pieces=small_pieces),
        grid=(depth, D // tc),
        in_specs=[pl.BlockSpec((None, n_in, tc), lambda l, c: (l, 0, c))],
        out_specs=[pl.BlockSpec((None, N_BIG, tc), lambda l, c: (l, 0, c)),
                   pl.BlockSpec((None, LANES, tc), lambda l, c: (l, 0, c))],
        out_shape=[jax.ShapeDtypeStruct((depth, N_BIG, D), BF16),
                   jax.ShapeDtypeStruct((depth, LANES, D), BF16)],
        compiler_params=pltpu.CompilerParams(
            dimension_semantics=("arbitrary", "arbitrary"), vmem_limit_bytes=VMEM_LIMIT),
        name="regroup",
    )(w_in_t)


_NT_DIMS = (((1,), (1,)), ((), ()))


def _proj_kernel(x_ref, g_ref, wb_ref, ws_ref, *rest, n_cast):
    cast_in = rest[:n_cast]
    p_ref, sm_ref, smt_ref = rest[n_cast:n_cast + 3]
    cast_out = rest[n_cast + 3:2 * n_cast + 3]
    u_sc = rest[2 * n_cast + 3]
    j = pl.program_id(1)

    @pl.when(j == 0)
    def _():
        x = x_ref[...]
        u = x * lax.rsqrt(jnp.mean(x * x, axis=-1, keepdims=True) + EPS) * g_ref[...]
        ub = u.astype(BF16)
        u_sc[...] = ub
        sm = lax.dot_general(ub, ws_ref[...], _NT_DIMS, preferred_element_type=F32)
        sm_ref[...] = sm
        smt = sm.T[:SM_USED, :]
        for c in range(smt_ref.shape[0]):
            smt_ref[c] = smt[:, c * CHUNK:(c + 1) * CHUNK]
        for src, dst in zip(cast_in, cast_out):
            dst[...] = src[...].astype(dst.dtype)

    p = lax.dot_general(u_sc[...], wb_ref[...], _NT_DIMS, preferred_element_type=F32)
    p_ref[...] = p.astype(p_ref.dtype)


def _proj(x2, g_row, w_big_t, w_small_t, to_cast, layer, tm, tn):
    T, D = x2.shape
    n_row = T // tm
    cast_in_specs, cast_out_specs, cast_out_shapes = [], [], []
    for w in to_cast:
        _, R, C = w.shape
        assert R % (n_row * BF16_ROWS) == 0
        cast_in_specs.append(pl.BlockSpec((None, R // n_row, C), lambda i, j: (layer, i, 0)))
        cast_out_specs.append(pl.BlockSpec((R // n_row, C), lambda i, j: (i, 0)))
        cast_out_shapes.append(jax.ShapeDtypeStruct((R, C), BF16))
    return pl.pallas_call(
        functools.partial(_proj_kernel, n_cast=len(to_cast)),
        grid=(n_row, N_BIG // tn),
        in_specs=[
            pl.BlockSpec((tm, D), lambda i, j: (i, 0)),
            pl.BlockSpec((1, D), lambda i, j: (0, 0)),
            pl.BlockSpec((None, tn, D), lambda i, j: (layer, j, 0)),
            pl.BlockSpec((None, LANES, D), lambda i, j: (layer, 0, 0)),
        ] + cast_in_specs,
        out_specs=[
            pl.BlockSpec((tm, tn), lambda i, j: (i, j)),
            pl.BlockSpec((tm, LANES), lambda i, j: (i, 0)),
            pl.BlockSpec((tm // CHUNK, SM_USED, CHUNK), lambda i, j: (i, 0, 0)),
        ] + cast_out_specs,
        out_shape=[
            jax.ShapeDtypeStruct((T, N_BIG), BF16),
            jax.ShapeDtypeStruct((T, LANES), F32),
            jax.ShapeDtypeStruct((T // CHUNK, SM_USED, CHUNK), F32),
        ] + cast_out_shapes,
        scratch_shapes=[pltpu.VMEM((tm, D), BF16)],
        compiler_params=pltpu.CompilerParams(
            dimension_semantics=("arbitrary", "arbitrary"), vmem_limit_bytes=VMEM_LIMIT),
        name="proj",
    )(x2, g_row, w_big_t, w_small_t, *to_cast)


def _unit_lower_inverse_minus_eye(a, blk_masks):
    a0 = a * blk_masks[0]
    n = -a0
    pw = _dot(a0, a0)
    yield
    span = 2
    while span < SOLVE_BASE:
        n_pw = _dot(n, pw)
        pw_next = _dot(pw, pw) if 2 * span < SOLVE_BASE else None
        yield
        n = n + pw + n_pw
        pw = pw_next
        span *= 2
    for m in blk_masks[1:]:
        c = a * m
        n_c = _dot(n, c)
        yield
        t = c + n_c
        t_n = _dot(t, n)
        yield
        n = n - t - t_n
    return n


def _mixers_kernel(pm_ref, pg_ref, pd_ref, sm_ref, smt_ref,
                   bifr_ref, bifc_ref, ghm_ref,
                   wlr_ref, bg_ref, ghl_ref, segm_ref, lvmask_ref, lvsign_ref,
                   cw_ref, arow_ref, acol_ref, drow_ref, dcol_ref, ghd_ref, blk_ref,
                   tri_ref, trit_ref,
                   out_ref,
                   c_sc, n_sc, m_sc, st_sc, s_sc, tail_sc, zbuf_sc):
    L = CHUNK
    W = MIX_W
    tb = pm_ref.shape[0]
    nchunk = tb // L
    nlev = lvmask_ref.shape[0]
    nlow = segm_ref.shape[0] // L - 1
    blk_masks = [blk_ref[i] for i in range(blk_ref.shape[0])]

    @pl.when(pl.program_id(1) == 0)
    def _():
        c_sc[...] = jnp.zeros_like(c_sc)
        n_sc[...] = jnp.zeros_like(n_sc)
        m_sc[...] = jnp.zeros_like(m_sc)
        st_sc[...] = jnp.zeros_like(st_sc)
        s_sc[...] = jnp.zeros_like(s_sc)
        tail_sc[...] = jnp.zeros_like(tail_sc)

    row_i = lax.broadcasted_iota(jnp.int32, (L, L), 0)
    col_i = lax.broadcasted_iota(jnp.int32, (L, L), 1)
    causal = col_i <= row_i
    sm_col = lax.broadcasted_iota(jnp.int32, (L, LANES), 1)
    is_lf_c = (sm_col >= SM_LF) & (sm_col < SM_LF + N_HEADS)
    is_da_c = (sm_col >= SM_DA) & (sm_col < SM_DA + N_HEADS)
    smt_row = lax.broadcasted_iota(jnp.int32, (SM_USED, L), 0)
    is_lf_r = (smt_row >= SM_LF) & (smt_row < SM_LF + N_HEADS)
    is_da_r = (smt_row >= SM_DA) & (smt_row < SM_DA + N_HEADS)

    def chunk(ci, carry):
        r0 = pl.multiple_of(ci * L, L)
        rows = pl.ds(r0, L)
        prev_rows = pl.ds(pl.multiple_of(jnp.maximum(r0 - BF16_ROWS, 0), BF16_ROWS), BF16_ROWS)

        sm = sm_ref[rows, :]
        tc = sm + bifr_ref[...]
        gdec_c = -jnp.exp(arow_ref[...]) * _softplus(sm + drow_ref[...])
        gc = jnp.where(is_lf_c, _log_sigmoid(tc), jnp.where(is_da_c, gdec_c, 0.0)) * LOG2E
        bc = _sel_rows(tri_ref[...], gc, 3)
        tc = tc * LOG2E
        smt = smt_ref[ci]
        tr = smt + bifc_ref[...]
        gdec_r = -jnp.exp(acol_ref[...]) * _softplus(smt + dcol_ref[...])
        gr = jnp.where(is_lf_r, _log_sigmoid(tr), jnp.where(is_da_r, gdec_r, 0.0)) * LOG2E
        br = _sel_cols(gr, trit_ref[...], 3)
        tr = tr * LOG2E
        beta_all = pltpu.roll(_sigmoid(sm), SM_DA - SM_BETA, 1)
        eb_all = jnp.exp2(bc)
        beb_all = beta_all * eb_all
        kdec_all = jnp.exp2(bc[L - 1:L, :] - bc)
        gpre = _dot(sm, wlr_ref[...]) + bg_ref[...]
        glog_all = _log_sigmoid(gpre) * (LOG2E / GLA_TAU)

        def mlstm_head(h):
            cols = slice(h * HEAD_DIM, (h + 1) * HEAD_DIM)
            q = pm_ref[rows, h * HEAD_DIM:(h + 1) * HEAD_DIM]
            k = pm_ref[rows, W + h * HEAD_DIM:W + (h + 1) * HEAD_DIM]
            v = pm_ref[rows, 2 * W + h * HEAD_DIM:2 * W + (h + 1) * HEAD_DIM]
            b_c = bc[:, SM_LF + h:SM_LF + h + 1]
            li_c = tc[:, SM_LI + h:SM_LI + h + 1]
            b_r = br[SM_LF + h:SM_LF + h + 1, :]
            li_r = tr[SM_LI + h:SM_LI + h + 1, :]
            m_old = m_sc[h][0:1, 0:1]
            n_old = n_sc[h][0:1, :]
            c_old = c_sc[h]

            s_qk = _dot_nt(q, k)
            q_c = _dot(q, c_old)
            yield
            d = jnp.where(causal, b_c + (li_r - b_r), NEG_BIG)
            inter = b_c + m_old
            m_t = jnp.maximum(inter, jnp.max(d, axis=1, keepdims=True))
            p = jnp.exp2(d - m_t) * s_qk
            w_inter = jnp.exp2(inter - m_t)
            b_last = b_c[L - 1:L, :]
            g_s = b_last - b_c + li_c
            m_new = jnp.maximum(b_last + m_old, jnp.max(g_s, axis=0, keepdims=True))
            w_s = jnp.exp2(g_s - m_new)
            w_old = jnp.exp2(b_last + m_old - m_new)
            kw = k.astype(F32) * w_s
            pv = _dot(p, v)
            kv = _dot_tn(kw, v)
            yield
            num = pv + w_inter * q_c
            den = (jnp.sum(p, axis=1, keepdims=True)
                   + w_inter * jnp.sum(q.astype(F32) * n_old, axis=1, keepdims=True))
            hout = num / jnp.maximum(jnp.abs(den), jnp.exp2(-m_t))
            c_sc[h] = w_old * c_old + kv
            n_new = w_old * n_old + jnp.sum(kw, axis=0, keepdims=True)
            n_sc[h] = jnp.broadcast_to(n_new, (SUBLANES, LANES))
            m_sc[h] = jnp.broadcast_to(m_new, (SUBLANES, LANES))
            ogate = pm_ref[rows, 3 * W + h * HEAD_DIM:3 * W + (h + 1) * HEAD_DIM].astype(F32)
            gated = _sigmoid_of_twice(ogate) * hout
            out_ref[rows, cols] = _head_rmsnorm(gated, ghm_ref[:, cols]).astype(out_ref.dtype)

        def gla_head(h):
            cols = slice(h * HEAD_DIM, (h + 1) * HEAD_DIM)
            qb = pg_ref[rows, h * HEAD_DIM:(h + 1) * HEAD_DIM]
            kb = pg_ref[rows, W + h * HEAD_DIM:W + (h + 1) * HEAD_DIM]
            q = qb.astype(F32)
            k = kb.astype(F32)
            v = pg_ref[rows, 2 * W + h * HEAD_DIM:2 * W + (h + 1) * HEAD_DIM]
            e = _sel_rows(segm_ref[...], glog_all[:, cols], 2)
            yield
            bsum = e[nlow * L:(nlow + 1) * L]
            st_old = st_sc[h]
            q_st = _dot_nt(q * jnp.exp2(bsum), st_old)
            vk = _dot_tn(v, k * jnp.exp2(bsum[L - 1:L, :] - bsum))
            yield
            st_sc[h] = jnp.exp2(bsum[L - 1:L, :]) * st_old + vk
            a = None
            for lv in range(nlev):
                if lv < nlow:
                    lw = e[lv * L:(lv + 1) * L]
                else:
                    half = 1 << lv
                    b3 = bsum.reshape(L // (2 * half), 2 * half, HEAD_DIM)
                    lw = (b3 - b3[:, half - 1:half, :]).reshape(L, HEAD_DIM) * lvsign_ref[lv - nlow]
                ex = jnp.exp2(lw).astype(BF16)
                a_lv = lvmask_ref[lv] * _dot_nt(qb * ex, kb * ex)
                a = a_lv if a is None else a + a_lv
                yield
            dqk = jnp.sum(q * k, axis=1, keepdims=True)
            o = _dot(a, v) + dqk * v.astype(F32) + q_st
            yield
            rgate = pg_ref[rows, 3 * W + h * HEAD_DIM:3 * W + (h + 1) * HEAD_DIM].astype(F32)
            hn = _head_rmsnorm(o, ghl_ref[:, cols]) * _silu_of_twice(rgate)
            out_ref[rows, W + h * HEAD_DIM:W + (h + 1) * HEAD_DIM] = hn.astype(out_ref.dtype)

        def gdn_conv(piece, h, l2_mult):
            c0 = piece * W + h * HEAD_DIM
            xb = pd_ref[rows, c0:c0 + HEAD_DIM]
            x = xb.astype(F32)
            w = cw_ref[:, c0:c0 + HEAD_DIM]
            prev = jnp.where(ci == 0, tail_sc[:, c0:c0 + HEAD_DIM].astype(F32),
                             pd_ref[prev_rows, c0:c0 + HEAD_DIM].astype(F32))
            zb = zbuf_sc.at[piece * N_HEADS + h]
            zb[0:SUBLANES, :] = prev[BF16_ROWS - SUBLANES:, :]
            zb[SUBLANES:SUBLANES + L, :] = x
            y = x * w[CONV_K - 1:CONV_K, :]
            for dlt in range(1, CONV_K):
                y = y + zb[SUBLANES - dlt:SUBLANES - dlt + L, :] * w[CONV_K - 1 - dlt:CONV_K - dlt, :]
            y = _silu_of_twice(y)
            if l2_mult is not None:
                y = y * (lax.rsqrt(jnp.sum(y * y, axis=-1, keepdims=True) + EPS) * l2_mult)
            return y

        def gdn_head(h):
            cols = slice(h * HEAD_DIM, (h + 1) * HEAD_DIM)
            lane = slice(SM_DA + h, SM_DA + h + 1)
            q = gdn_conv(0, h, HEAD_DIM ** -0.5)
            k = gdn_conv(1, h, 1.0)
            v = gdn_conv(2, h, None)
            b_c = bc[:, lane]
            b_r = br[SM_DA + h:SM_DA + h + 1, :]
            beta = beta_all[:, lane]
            s_old = s_sc[h]

            kk = _dot_nt(k, k)
            yield
            decay = jnp.where(causal, jnp.exp2(jnp.minimum(b_c - b_r, 0.0)), 0.0)
            a = (beta * decay) * kk
            n = yield from _unit_lower_inverse_minus_eye(a, blk_masks)
            rhs = jnp.concatenate([beb_all[:, lane] * k, beta * v], axis=1)
            n_rhs = _dot(n, rhs)
            qk_raw = _dot_nt(q, k)
            yield
            wu = rhs + n_rhs
            w, u = wu[:, :HEAD_DIM], wu[:, HEAD_DIM:]
            ws_qs = _dot(jnp.concatenate([w, q * eb_all[:, lane]], axis=0), s_old)
            yield
            v_new = u - ws_qs[:L]
            qk_v = _dot(qk_raw * decay, v_new)
            k_v = _dot_tn(k * kdec_all[:, lane], v_new)
            yield
            o = ws_qs[L:] + qk_v
            s_sc[h] = eb_all[L - 1:L, lane] * s_old + k_v
            zgate = pd_ref[rows, 3 * W + h * HEAD_DIM:3 * W + (h + 1) * HEAD_DIM].astype(F32)
            hn = _head_rmsnorm(o, ghd_ref[:, cols]) * _silu_of_twice(zgate)
            out_ref[rows, 2 * W + h * HEAD_DIM:2 * W + (h + 1) * HEAD_DIM] = hn.astype(out_ref.dtype)

        gens = [f(h) for f in (gdn_head, gla_head, mlstm_head) for h in range(N_HEADS)]
        _round_robin(gens)
        return carry

    lax.fori_loop(0, nchunk, chunk, 0)
    tail_sc[...] = pd_ref[tb - BF16_ROWS:tb, 0:3 * W]


def _mixers(P, sm, smt, consts, B, S, tb):
    T = B * S
    nblk = S // tb
    row_blk = lambda b, c: b * nblk + c
    in_specs = [pl.BlockSpec((tb, GROUP_W), lambda b, c, g=g: (row_blk(b, c), g)) for g in range(3)]
    in_specs.append(pl.BlockSpec((tb, LANES), lambda b, c: (row_blk(b, c), 0)))
    in_specs.append(pl.BlockSpec((tb // CHUNK, SM_USED, CHUNK), lambda b, c: (row_blk(b, c), 0, 0)))
    in_specs += [_full_spec(a.shape) for a in consts]
    state = pltpu.VMEM((N_HEADS, HEAD_DIM, HEAD_DIM), F32)
    small = pltpu.VMEM((N_HEADS, SUBLANES, LANES), F32)
    return pl.pallas_call(
        _mixers_kernel,
        grid=(B, nblk),
        in_specs=in_specs,
        out_specs=pl.BlockSpec((tb, N_BRANCH * MIX_W), lambda b, c: (row_blk(b, c), 0)),
        out_shape=jax.ShapeDtypeStruct((T, N_BRANCH * MIX_W), BF16),
        scratch_shapes=[state, small, small, state, state,
                        pltpu.VMEM((BF16_ROWS, 3 * MIX_W), BF16),
                        pltpu.VMEM((3 * N_HEADS, SUBLANES + CHUNK, HEAD_DIM), F32)],
        compiler_params=pltpu.CompilerParams(
            dimension_semantics=("arbitrary", "arbitrary"), vmem_limit_bytes=VMEM_LIMIT),
        name="mixers",
    )(P, P, P, sm, smt, *consts)


def _merge_kernel(x_ref, h_ref, gates_ref, wup_ref, wout_ref, out_ref):
    d = x_ref.shape[1]
    acc = None
    for n in range(N_BRANCH):
        up = jnp.dot(h_ref[:, n * MIX_W:(n + 1) * MIX_W], wup_ref[n * MIX_W:(n + 1) * MIX_W, :],
                     preferred_element_type=F32)
        t = _sigmoid(gates_ref[:, n * d:(n + 1) * d].astype(F32)) * up
        acc = t if acc is None else acc + t
    out_ref[...] = x_ref[...] + jnp.dot(acc.astype(BF16), wout_ref[...], preferred_element_type=F32)


def _merge(x2, h, P, wup, wout, tm):
    T, D = x2.shape
    gate_blk = (3 * GROUP_W) // (N_BRANCH * D)
    row = lambda i: (i, 0)
    return pl.pallas_call(
        _merge_kernel,
        grid=(T // tm,),
        in_specs=[pl.BlockSpec((tm, D), row), pl.BlockSpec((tm, N_BRANCH * MIX_W), row),
                  pl.BlockSpec((tm, N_BRANCH * D), lambda i: (i, gate_blk)),
                  _full_spec((N_BRANCH * MIX_W, D)), _full_spec((D, D))],
        out_specs=pl.BlockSpec((tm, D), row),
        out_shape=jax.ShapeDtypeStruct((T, D), F32),
        compiler_params=pltpu.CompilerParams(
            dimension_semantics=("arbitrary",), vmem_limit_bytes=VMEM_LIMIT),
        name="merge",
    )(x2, h, P, wup, wout)


def _rmsnorm(x, g_row):
    return x * lax.rsqrt(jnp.mean(x * x, axis=-1, keepdims=True) + EPS) * g_row


def _mlp_kernel(x_ref, g_ref, w1_ref, w2_ref, gout_ref, out_ref, u_sc, acc_sc, *, norm_out):
    f = pl.program_id(1)

    @pl.when(f == 0)
    def _():
        u_sc[...] = _rmsnorm(x_ref[...], g_ref[...]).astype(BF16)
        acc_sc[...] = jnp.zeros_like(acc_sc)

    hmid = jnp.maximum(jnp.dot(u_sc[...], w1_ref[...], preferred_element_type=F32), 0.0)
    acc_sc[...] += jnp.dot((hmid * hmid).astype(BF16), w2_ref[...], preferred_element_type=F32)

    @pl.when(f == pl.num_programs(1) - 1)
    def _():
        y = x_ref[...] + acc_sc[...]
        out_ref[...] = _rmsnorm(y, gout_ref[...]) if norm_out else y


def _mlp(x2, g_row, w1, w2, tm, tf, g_out_row=None):
    T, D = x2.shape
    F = w1.shape[1]
    norm_out = g_out_row is not None
    return pl.pallas_call(
        functools.partial(_mlp_kernel, norm_out=norm_out),
        grid=(T // tm, F // tf),
        in_specs=[pl.BlockSpec((tm, D), lambda i, f: (i, 0)),
                  pl.BlockSpec((1, D), lambda i, f: (0, 0)),
                  pl.BlockSpec((D, tf), lambda i, f: (0, f)),
                  pl.BlockSpec((tf, D), lambda i, f: (f, 0)),
                  pl.BlockSpec((1, D), lambda i, f: (0, 0))],
        out_specs=pl.BlockSpec((tm, D), lambda i, f: (i, 0)),
        out_shape=jax.ShapeDtypeStruct((T, D), F32),
        scratch_shapes=[pltpu.VMEM((tm, D), BF16), pltpu.VMEM((tm, D), F32)],
        compiler_params=pltpu.CompilerParams(
            dimension_semantics=("arbitrary", "arbitrary"), vmem_limit_bytes=VMEM_LIMIT),
        name="mlp",
    )(x2, g_row, w1, w2, g_out_row if norm_out else g_row)


def _tile(n, want):
    t = min(n, want)
    while n % t:
        t //= 2
    return t


def _pad_rows(vals, offset, width=LANES):
    out = jnp.zeros((vals.shape[0], 1, width), F32)
    return out.at[:, 0, offset:offset + vals.shape[1]].set(vals.astype(F32))


def _as_cols(rows):
    return jnp.broadcast_to(rows[:, 0, :SM_USED, None], (rows.shape[0], SM_USED, CHUNK))


def kernel(x, w_in, b_if, w_gla_lr, b_gla, conv_gdn, a_log, dt_bias, g_norm_mix, g_norm_mlp,
           g_head_mlstm, g_head_gla, g_head_gdn, w_up, w_out, w_mlp_in, w_mlp_out, g_final):
    B, S, D = x.shape
    depth = w_in.shape[0]
    T = B * S
    H, W = N_HEADS, MIX_W
    assert S % CHUNK == 0 and D == D_MODEL

    tri_np, lvm_np, lvmask_np = _np_consts(CHUNK)
    tri = jnp.asarray(np.tile(tri_np, (1, 3)), BF16)
    trit = jnp.asarray(np.tile(tri_np.T, (3, 1)), BF16)
    n_low = int(np.log2(SUBLANES))
    segm = jnp.asarray(np.tile(np.concatenate(list(lvm_np[:n_low]) + [tri_np], axis=0), (1, 2)), BF16)
    lvmask = jnp.asarray(lvmask_np, F32)
    ti = np.arange(CHUNK)[:, None]
    ui = np.arange(CHUNK)[None, :]
    lvsign = jnp.asarray(np.stack([
        np.broadcast_to(np.where((ti // (1 << lv)) % 2 == 1, 1.0, -1.0), (CHUNK, HEAD_DIM))
        for lv in range(n_low, lvmask_np.shape[0])]), F32)
    blk_list = [((ti // SOLVE_BASE) == (ui // SOLVE_BASE)) & (ui < ti)]
    c = SOLVE_BASE
    while c < CHUNK:
        blk_list.append(((ti // (2 * c)) == (ui // (2 * c))) & ((ti // c) != (ui // c)))
        c *= 2
    blk = jnp.asarray(np.stack(blk_list), F32)

    sizes = [W, W, W, W, H, H, W, W, W, W, GLA_RANK, W, W, W, W, H, H, N_BRANCH * D]
    offs = [int(o) for o in np.concatenate([[0], np.cumsum(sizes)])]
    q_scale = HEAD_DIM ** -0.5
    big_ids = (0, 1, 2, 3, 6, 7, 8, 9, 11, 12, 13, 14, 17)
    pieces, dst = [], 0
    scales = {0: q_scale, 6: q_scale, 3: 0.5, 9: 0.5, 14: 0.5}
    for i in big_ids:
        pieces.append((offs[i], dst, sizes[i], scales.get(i, 1.0)))
        dst += sizes[i]
    small_pieces = ((offs[4], 2 * H), (offs[10], GLA_RANK), (offs[15], 2 * H))
    assert all(p[0] % SUBLANES == 0 for p in pieces) and all(p[0] % SUBLANES == 0 for p in small_pieces)
    w_big, w_small = _regroup(jnp.swapaxes(w_in, 1, 2), tuple(pieces), small_pieces, LANES)
    to_cast = (w_up.reshape(depth, N_BRANCH * W, D), w_out, w_mlp_in, w_mlp_out)

    conv_half = 0.5 * conv_gdn.astype(F32)
    bif_rows = _pad_rows(b_if, SM_LI)
    a_rows = _pad_rows(a_log, SM_DA)
    dt_rows = _pad_rows(dt_bias, SM_DA)
    bif_cols, a_cols, dt_cols = _as_cols(bif_rows), _as_cols(a_rows), _as_cols(dt_rows)
    wlr = jnp.zeros((depth, LANES, W), F32).at[:, SM_LLR:SM_LLR + GLA_RANK].set(w_gla_lr).astype(BF16)

    tm_proj = _tile(T, 1024)
    tn_proj = 3072
    tb = _tile(S, 1024)
    tm_merge = _tile(T, 1024)
    tm_mlp = _tile(T, 1024)
    tf_mlp = _tile(w_mlp_in.shape[2], 2048)

    x2 = x.reshape(T, D)
    for l in range(depth):
        P, sm, smt, w_up_b, w_out_b, w_mlp_in_b, w_mlp_out_b = _proj(
            x2, g_norm_mix[l].reshape(1, D), w_big, w_small, to_cast, l, tm_proj, tn_proj)
        consts = (bif_rows[l], bif_cols[l], g_head_mlstm[l].reshape(1, W),
                  wlr[l], b_gla[l].reshape(1, W), g_head_gla[l].reshape(1, W), segm, lvmask, lvsign,
                  conv_half[l], a_rows[l], a_cols[l], dt_rows[l], dt_cols[l],
                  g_head_gdn[l].reshape(1, W), blk, tri, trit)
        h = _mixers(P, sm, smt, consts, B, S, tb)
        x2 = _merge(x2, h, P, w_up_b, w_out_b, tm_merge)
        x2 = _mlp(x2, g_norm_mlp[l].reshape(1, D), w_mlp_in_b, w_mlp_out_b, tm_mlp, tf_mlp,
                  g_final.reshape(1, D) if l == depth - 1 else None)
    return x2.reshape(B, S, D)
```
